```python
import jax, jax.numpy as jnp
from jax import lax
import numpy as np

D_MODEL = 2048
BATCH = 4
SEQ = 2048
DEPTH = 2
DEC_BATCH = 8
DEC_SEQ = 8
PAST_LEN = 16384
PAGE_SIZE = 128

N_META = 16
EPS = 1e-6
H_A = 4
DK_A = 128
DV_A = 256
LOWRANK_A = 16
GATE_TAU = 16.0
GLA_CHUNK = 64
C_B = 1024
CONV_W = 31
H_C = 16
DH_C = D_MODEL // H_C
SB_BLOCK = 128
SB_BIAS_INIT = -6.5
PEER_HEADS = 8
PEER_NKEYS = 128
PEER_N = PEER_NKEYS * PEER_NKEYS
PEER_DQ = 256
PEER_TOPK = 16
PEER_BLOCK = 64
PEER_V_SCALE = 0.05

N_EVEN = (DEPTH + 1) // 2
N_ODD = DEPTH // 2
QA = H_A * DK_A
VA = H_A * DV_A
PROJ_EVEN = 2 * QA + 2 * VA + LOWRANK_A + 2 * C_B
MIX_EVEN = VA + C_B

kernel_name = "hybrid_gla_conformer_stickbreak_peer_step"


def rmsnorm(x, g):
    xf = x.astype(jnp.float32)
    y = xf * lax.rsqrt(jnp.mean(xf * xf, axis=-1, keepdims=True) + EPS)
    return (y * g.astype(jnp.float32)).astype(x.dtype)


def layernorm(x, g, b):
    xf = x.astype(jnp.float32)
    mu = jnp.mean(xf, axis=-1, keepdims=True)
    var = jnp.mean(jnp.square(xf - mu), axis=-1, keepdims=True)
    return ((xf - mu) * lax.rsqrt(var + EPS) * g.astype(jnp.float32) + b.astype(jnp.float32)).astype(x.dtype)


def gla_chunked(q, k, v, loga, s0, chunk):
    B, L, H, _ = q.shape
    nc = L // chunk
    to_chunks = lambda t: jnp.moveaxis(t.reshape(B, nc, chunk, H, t.shape[-1]), 1, 0)
    causal = jnp.tril(jnp.ones((chunk, chunk), bool))[None, :, :, None, None]

    def step(S, inp):
        qi, ki, vi, ai = inp
        b = jnp.cumsum(ai, axis=1)
        diff = b[:, :, None] - b[:, None, :]
        decay = jnp.exp(jnp.where(causal, diff, -jnp.inf))
        att = jnp.einsum('bthd,bshd,btshd->bhts', qi, ki, decay)
        o = (jnp.einsum('bhts,bshv->bthv', att, vi)
             + jnp.einsum('bthd,bhdv->bthv', qi * jnp.exp(b), S))
        b_last = b[:, -1]
        k_dec = ki * jnp.exp(b_last[:, None] - b)
        S = jnp.exp(b_last)[..., None] * S + jnp.einsum('bshd,bshv->bhdv', k_dec, vi)
        return S, o

    S, o = lax.scan(step, s0, (to_chunks(q), to_chunks(k), to_chunks(v), to_chunks(loga)))
    return jnp.moveaxis(o, 0, 1).reshape(B, L, H, v.shape[-1]), S


def causal_depthwise_conv(u, buf, w, b):
    ucat = jnp.concatenate([buf.astype(u.dtype), u], axis=1)
    y = lax.conv_general_dilated(ucat, w[:, None, :].astype(u.dtype), (1,), 'VALID',
                                 dimension_numbers=('NWC', 'WIO', 'NWC'),
                                 feature_group_count=u.shape[-1])
    return y + b.astype(u.dtype), ucat[:, -(CONV_W - 1):]


def even_mixer(h, s0, buf, segments, w_in, w_lr, b_lr, gla_g, cw, cb, cng, cnb, w_out):
    f32 = jnp.float32
    B, L, _ = h.shape
    p = h @ w_in
    cuts = [int(c) for c in np.cumsum([QA, QA, VA, VA, LOWRANK_A, C_B])]
    q, k, v, g, lr, ga, gb = jnp.split(p, cuts, axis=-1)
    loga = jax.nn.log_sigmoid((lr @ w_lr + b_lr).astype(f32)) / GATE_TAU
    heads = lambda t, d: t.astype(f32).reshape(B, L, H_A, d)
    q = heads(q, DK_A) * (DK_A ** -0.5)
    k = heads(k, DK_A)
    v = heads(v, DV_A)
    loga = loga.reshape(B, L, H_A, DK_A)
    S = s0.astype(f32)
    outs = []
    off = 0
    for seg_len, chunk in segments:
        sl = slice(off, off + seg_len)
        o, S = gla_chunked(q[:, sl], k[:, sl], v[:, sl], loga[:, sl], S, chunk)
        outs.append(o)
        off += seg_len
    o = jnp.concatenate(outs, axis=1)
    o_a = (rmsnorm(o, gla_g) * jax.nn.silu(heads(g, DV_A))).reshape(B, L, VA)
    u = ga * jax.nn.sigmoid(gb)
    c, new_buf = causal_depthwise_conv(u, buf, cw, cb)
    o_b = jax.nn.silu(layernorm(c, cng, cnb).astype(f32))
    y = jnp.concatenate([o_a.astype(h.dtype), o_b.astype(h.dtype)], axis=-1) @ w_out
    return y, S, new_buf


def stick_breaking_weights(z, mask):
    sp = jnp.where(mask, jax.nn.softplus(z), 0.0)
    rc = lax.cumsum(sp, axis=z.ndim - 1, reverse=True)
    after = jnp.concatenate([rc[..., 1:], jnp.zeros_like(rc[..., :1])], axis=-1)
    return jnp.where(mask, jnp.exp(jax.nn.log_sigmoid(z) - after), 0.0)


def qkv_heads(h, w_qkv):
    B, L, _ = h.shape
    q, k, v = jnp.split(h @ w_qkv, 3, axis=-1)
    r = lambda t: t.reshape(B, L, H_C, DH_C)
    return r(q) * (DH_C ** -0.5), r(k), r(v)


def sb_block(qb, qpos, kb, vb, kpos, bias):
    z = jnp.einsum('bqhd,bkhd->bhqk', qb, kb).astype(jnp.float32) + bias.astype(jnp.float32)[None, :, None, None]
    A = stick_breaking_weights(z, kpos[None, :] < qpos[:, None])
    return jnp.einsum('bhqk,bkhd->bqhd', A, vb.astype(jnp.float32))


def sb_prompt(h, w_qkv, w_o, bias):
    B, L, _ = h.shape
    q, k, v = qkv_heads(h, w_qkv)
    pos = jnp.arange(L)
    o_meta = sb_block(q[:, :N_META], pos[:N_META], k[:, :N_META], v[:, :N_META], pos[:N_META], bias)
    nb = (L - N_META) // SB_BLOCK
    qr = jnp.swapaxes(q[:, N_META:].reshape(B, nb, SB_BLOCK, H_C, DH_C), 0, 1)
    pr = pos[N_META:].reshape(nb, SB_BLOCK)
    o_real = lax.map(lambda a: sb_block(a[0], a[1], k, v, pos, bias), (qr, pr))
    o_real = jnp.swapaxes(o_real, 0, 1).reshape(B, L - N_META, H_C, DH_C)
    o = jnp.concatenate([o_meta, o_real], axis=1).reshape(B, L, H_C * DH_C)
    return o.astype(h.dtype) @ w_o, k, v


def sb_sample(h, cache_k, cache_v, page_table, w_qkv, w_o, bias):
    B, T, _ = h.shape
    q, k, v = qkv_heads(h, w_qkv)
    n_pages = page_table.shape[1]
    past = n_pages * PAGE_SIZE
    k_past = cache_k[page_table].reshape(B, past, H_C, DH_C)
    v_past = cache_v[page_table].reshape(B, past, H_C, DH_C)
    z = jnp.concatenate([jnp.einsum('bqhd,bkhd->bhqk', q, k_past.astype(q.dtype)),
                         jnp.einsum('bqhd,bkhd->bhqk', q, k)], axis=-1).astype(jnp.float32)
    z = z + bias.astype(jnp.float32)[None, :, None, None]
    qpos = past + jnp.arange(T)
    kpos = jnp.arange(past + T)
    A = stick_breaking_weights(z, kpos[None, :] < qpos[:, None])
    o = (jnp.einsum('bhqk,bkhd->bqhd', A[..., :past], v_past.astype(jnp.float32))
         + jnp.einsum('bhqk,bkhd->bqhd', A[..., past:], v.astype(jnp.float32)))
    return o.reshape(B, T, H_C * DH_C).astype(h.dtype) @ w_o, k, v


def peer_ffn(h, w_q, sub_keys, u_tab, v_tab):
    f32 = jnp.float32
    B, L, D = h.shape
    T = B * L
    x = h.reshape(T, D)
    q = (x @ w_q).astype(f32).reshape(T, PEER_HEADS, 2, PEER_DQ // 2)
    s = jnp.einsum('thcd,hcnd->thcn', q, sub_keys.astype(f32))
    sv, si = lax.top_k(s, PEER_TOPK)
    cand = (sv[:, :, 0, :, None] + sv[:, :, 1, None, :]).reshape(T, PEER_HEADS, -1)
    cidx = (si[:, :, 0, :, None] * PEER_NKEYS + si[:, :, 1, None, :]).reshape(T, PEER_HEADS, -1)
    best, pos = lax.top_k(cand, PEER_TOPK)
    eidx = jnp.take_along_axis(cidx, pos, axis=-1)
    gate = jax.nn.softmax(best, axis=-1)
    nblk = -(-T // PEER_BLOCK)
    pad = nblk * PEER_BLOCK - T
    xb = jnp.pad(x, ((0, pad), (0, 0))).reshape(nblk, PEER_BLOCK, D)
    eb = jnp.pad(eidx, ((0, pad), (0, 0), (0, 0))).reshape(nblk, PEER_BLOCK, PEER_HEADS, PEER_TOPK)
    gb = jnp.pad(gate, ((0, pad), (0, 0), (0, 0))).reshape(nblk, PEER_BLOCK, PEER_HEADS, PEER_TOPK)

    def expert_block(args):
        xi, ei, gi = args
        act = jax.nn.gelu(jnp.einsum('td,thkd->thk', xi, u_tab[ei]).astype(f32))
        return jnp.einsum('thk,thkd->td', (gi * act).astype(x.dtype), v_tab[ei])

    y = lax.map(expert_block, (xb, eb, gb)).reshape(nblk * PEER_BLOCK, D)[:T]
    return y.reshape(B, L, D)


def setup_inputs(seed: int = 0) -> dict:
    key = jax.random.key(seed)
    ks = jax.random.split(key, 32)
    f32 = jnp.float32
    n_pages = PAST_LEN // PAGE_SIZE
    n_pool = (5 * DEC_BATCH * n_pages) // 4
    nrm = lambda k, shape, s: jax.random.normal(k, shape, f32) * s
    gain = lambda k, shape: 1.0 + 0.01 * jax.random.normal(k, shape, f32)
    page_table = jax.random.permutation(ks[6], n_pool)[:DEC_BATCH * n_pages]
    page_table = page_table.reshape(DEC_BATCH, n_pages).astype(jnp.int32)
    return {
        "x_prompt": nrm(ks[0], (BATCH, SEQ, D_MODEL), 1.0),
        "x_sample": nrm(ks[1], (DEC_BATCH, DEC_SEQ, D_MODEL), 1.0),
        "state_gla": nrm(ks[2], (N_EVEN, DEC_BATCH, H_A, DK_A, DV_A), 1.0),
        "state_conv": nrm(ks[3], (N_EVEN, DEC_BATCH, CONV_W - 1, C_B), 0.5),
        "cache_k": nrm(ks[4], (N_ODD, n_pool, PAGE_SIZE, H_C, DH_C), 1.0),
        "cache_v": nrm(ks[5], (N_ODD, n_pool, PAGE_SIZE, H_C, DH_C), 1.0),
        "page_table": page_table,
        "meta_tokens": nrm(ks[7], (N_META, D_MODEL), 1.0),
        "norm_mix": gain(ks[8], (DEPTH, D_MODEL)),
        "norm_ffn": gain(ks[9], (DEPTH, D_MODEL)),
        "norm_final": gain(ks[10], (D_MODEL,)),
        "w_in_even": nrm(ks[11], (N_EVEN, D_MODEL, PROJ_EVEN), D_MODEL ** -0.5),
        "w_gate_lr": nrm(ks[12], (N_EVEN, LOWRANK_A, QA), LOWRANK_A ** -0.5),
        "b_gate_lr": nrm(ks[13], (N_EVEN, QA), 0.01),
        "gla_norm": gain(ks[14], (N_EVEN, DV_A)),
        "conv_w": nrm(ks[15], (N_EVEN, CONV_W, C_B), CONV_W ** -0.5),
        "conv_b": nrm(ks[16], (N_EVEN, C_B), 0.01),
        "conv_norm_g": gain(ks[17], (N_EVEN, C_B)),
        "conv_norm_b": nrm(ks[18], (N_EVEN, C_B), 0.01),
        "w_out_even": nrm(ks[19], (N_EVEN, MIX_EVEN, D_MODEL), MIX_EVEN ** -0.5),
        "w_qkv_odd": nrm(ks[20], (N_ODD, D_MODEL, 3 * H_C * DH_C), D_MODEL ** -0.5),
        "w_out_odd": nrm(ks[21], (N_ODD, H_C * DH_C, D_MODEL), (H_C * DH_C) ** -0.5),
        "sb_bias": SB_BIAS_INIT + nrm(ks[26], (N_ODD, H_C), 0.3),
        "peer_wq": nrm(ks[22], (DEPTH, D_MODEL, PEER_HEADS * PEER_DQ), D_MODEL ** -0.5),
        "peer_keys": nrm(ks[23], (DEPTH, PEER_HEADS, 2, PEER_NKEYS, PEER_DQ // 2), (PEER_DQ // 2) ** -0.5),
        "peer_u": nrm(ks[24], (DEPTH, PEER_N, D_MODEL), D_MODEL ** -0.5),
        "peer_v": nrm(ks[25], (DEPTH, PEER_N, D_MODEL), PEER_V_SCALE),
    }


def reference(x_prompt, x_sample, state_gla, state_conv, cache_k, cache_v, page_table,
              meta_tokens, norm_mix, norm_ffn, norm_final, w_in_even, w_gate_lr, b_gate_lr,
              gla_norm, conv_w, conv_b, conv_norm_g, conv_norm_b, w_out_even, w_qkv_odd,
              w_out_odd, sb_bias, peer_wq, peer_keys, peer_u, peer_v):
    B = x_prompt.shape[0]
    xp = jnp.concatenate([jnp.broadcast_to(meta_tokens.astype(x_prompt.dtype)[None], (B, N_META, D_MODEL)),
                          x_prompt], axis=1)
    xs = x_sample
    Lp = xp.shape[1]
    Ts = xs.shape[1]
    gla_p, gla_s, conv_p, conv_s, k_p, v_p, k_s, v_s = [], [], [], [], [], [], [], []
    for layer in range(DEPTH):
        i = layer // 2
        hp = rmsnorm(xp, norm_mix[layer])
        hs = rmsnorm(xs, norm_mix[layer])
        if layer % 2 == 0:
            ew = (w_in_even[i], w_gate_lr[i], b_gate_lr[i], gla_norm[i], conv_w[i], conv_b[i],
                  conv_norm_g[i], conv_norm_b[i], w_out_even[i])
            s0 = jnp.zeros((B, H_A, DK_A, DV_A), jnp.float32)
            buf0 = jnp.zeros((B, CONV_W - 1, C_B), xp.dtype)
            yp, sp, cp = even_mixer(hp, s0, buf0, ((N_META, N_META), (Lp - N_META, GLA_CHUNK)), *ew)
            ys, ss, cs = even_mixer(hs, state_gla[i], state_conv[i], ((Ts, Ts),), *ew)
            gla_p.append(sp)
            gla_s.append(ss)
            conv_p.append(cp)
            conv_s.append(cs)
        else:
            yp, kp_, vp_ = sb_prompt(hp, w_qkv_odd[i], w_out_odd[i], sb_bias[i])
            ys, ks_, vs_ = sb_sample(hs, cache_k[i], cache_v[i], page_table, w_qkv_odd[i], w_out_odd[i], sb_bias[i])
            k_p.append(kp_)
            v_p.append(vp_)
            k_s.append(ks_)
            v_s.append(vs_)
        xp = xp + yp
        xs = xs + ys
        pw = (peer_wq[layer], peer_keys[layer], peer_u[layer], peer_v[layer])
        xp = xp + peer_ffn(rmsnorm(xp, norm_ffn[layer]), *pw)
        xs = xs + peer_ffn(rmsnorm(xs, norm_ffn[layer]), *pw)
    y_prompt = rmsnorm(xp, norm_final)[:, N_META:]
    y_sample = rmsnorm(xs, norm_final)
    return (y_prompt, y_sample, jnp.stack(gla_p), jnp.stack(gla_s), jnp.stack(conv_p), jnp.stack(conv_s),
            jnp.stack(k_p), jnp.stack(v_p), jnp.stack(k_s), jnp.stack(v_s))
```

```python
import functools

import jax
import jax.numpy as jnp
from jax import lax
from jax.experimental import pallas as pl
from jax.experimental.pallas import tpu as pltpu

F32 = jnp.float32
BF16 = jnp.bfloat16
HIGHEST = lax.Precision.HIGHEST

N_META = 16
EPS = 1e-6
H_A = 4
DK_A = 128
DV_A = 256
LOWRANK_A = 16
GATE_TAU = 16.0
C_B = 1024
CONV_W = 31
H_C = 16
DH_C = 128
PAGE_SIZE = 128
PEER_HEADS = 8
PEER_NKEYS = 128
PEER_DQ = 256
PEER_TOPK = 16
QA = H_A * DK_A
VA = H_A * DV_A

LANE = 128
SUBLANE = 8
HALO = 32
VMEM_LIMIT = 56 * 1024 * 1024

NT_DIMS = (((1,), (1,)), ((), ()))
TN_DIMS = (((0,), (0,)), ((), ()))


def _params(*sem):
    return pltpu.CompilerParams(dimension_semantics=sem, vmem_limit_bytes=VMEM_LIMIT)


def _softplus(x):
    return jnp.maximum(x, 0.0) + jnp.log1p(jnp.exp(-jnp.abs(x)))


def _log_sigmoid(x):
    return -_softplus(-x)


def _sigmoid(x):
    return 1.0 / (1.0 + jnp.exp(-x))


def _silu(x):
    return x * _sigmoid(x)


def _gelu_tanh(x):
    c = 0.7978845608028654
    return 0.5 * x * (1.0 + jnp.tanh(c * (x + 0.044715 * (x * x * x))))


def _rms_matmul_kernel(x_ref, g_ref, w_ref, o_ref, *rest, emit_h):
    h_ref = rest[-1]

    @pl.when(pl.program_id(1) == 0)
    def _():
        x = x_ref[...]
        ms = jnp.mean(x * x, axis=-1, keepdims=True)
        h = (x * lax.rsqrt(ms + EPS) * g_ref[...]).astype(BF16)
        h_ref[...] = h
        if emit_h:
            rest[0][...] = h

    o_ref[...] = jnp.dot(h_ref[...], w_ref[...], preferred_element_type=F32)


def rms_matmul(x, g, w, tm, tn, emit_h=False):
    T, D = x.shape
    N = w.shape[1]
    out_shape = [jax.ShapeDtypeStruct((T, N), F32)]
    out_specs = [pl.BlockSpec((tm, tn), lambda i, j: (i, j))]
    if emit_h:
        out_shape.append(jax.ShapeDtypeStruct((T, D), BF16))
        out_specs.append(pl.BlockSpec((tm, D), lambda i, j: (i, 0)))
    res = pl.pallas_call(
        functools.partial(_rms_matmul_kernel, emit_h=emit_h),
        grid=(T // tm, N // tn),
        in_specs=[pl.BlockSpec((tm, D), lambda i, j: (i, 0)),
                  pl.BlockSpec((1, D), lambda i, j: (0, 0)),
                  pl.BlockSpec((D, tn), lambda i, j: (0, j))],
        out_specs=out_specs,
        out_shape=out_shape,
        scratch_shapes=[pltpu.VMEM((tm, D), BF16)],
        compiler_params=_params("parallel", "arbitrary"),
        name="rms_matmul",
    )(x, g.reshape(1, D), w)
    return res if emit_h else res[0]


def _matmul_res_kernel(a_ref, w_ref, r_ref, o_ref):
    o_ref[...] = r_ref[...] + jnp.dot(a_ref[...], w_ref[...], preferred_element_type=F32)


def matmul_res(a, w, res, tm, tn):
    T, K = a.shape
    N = w.shape[1]
    return pl.pallas_call(
        _matmul_res_kernel,
        grid=(T // tm, N // tn),
        in_specs=[pl.BlockSpec((tm, K), lambda i, j: (i, 0)),
                  pl.BlockSpec((K, tn), lambda i, j: (0, j)),
                  pl.BlockSpec((tm, tn), lambda i, j: (i, j))],
        out_specs=pl.BlockSpec((tm, tn), lambda i, j: (i, j)),
        out_shape=jax.ShapeDtypeStruct((T, N), F32),
        compiler_params=_params("parallel", "arbitrary"),
        name="matmul_res",
    )(a, w, res)


def _rmsnorm_kernel(x_ref, g_ref, o_ref):
    x = x_ref[...]
    ms = jnp.mean(x * x, axis=-1, keepdims=True)
    o_ref[...] = x * lax.rsqrt(ms + EPS) * g_ref[...]


def rmsnorm_rows(x, g, tm):
    T, D = x.shape
    return pl.pallas_call(
        _rmsnorm_kernel,
        grid=(T // tm,),
        in_specs=[pl.BlockSpec((tm, D), lambda i: (i, 0)),
                  pl.BlockSpec((1, D), lambda i: (0, 0))],
        out_specs=pl.BlockSpec((tm, D), lambda i: (i, 0)),
        out_shape=jax.ShapeDtypeStruct((T, D), F32),
        compiler_params=_params("parallel"),
        name="rmsnorm_final",
    )(x, g.reshape(1, D))


def _transpose_add_kernel(x_ref, yt_ref, o_ref):
    o_ref[...] = x_ref[...] + yt_ref[...].T


def transpose_add(x, yt, tb):
    T, D = x.shape
    return pl.pallas_call(
        _transpose_add_kernel,
        grid=(T // tb,),
        in_specs=[pl.BlockSpec((tb, D), lambda i: (i, 0)),
                  pl.BlockSpec((D, tb), lambda i: (0, i))],
        out_specs=pl.BlockSpec((tb, D), lambda i: (i, 0)),
        out_shape=jax.ShapeDtypeStruct((T, D), F32),
        compiler_params=_params("parallel"),
        name="transpose_add",
    )(x, yt)


def _gla_kernel(q_ref, k_ref, v_ref, g_ref, lr_ref, wlr_ref, blr_ref, gn_ref, s0_ref,
                o_ref, sout_ref, S_ref, *, C, sb):
    c = pl.program_id(2)

    @pl.when(c == 0)
    def _():
        S_ref[...] = s0_ref[0, 0]

    x = jnp.dot(lr_ref[...].astype(BF16), wlr_ref[...], preferred_element_type=F32) + blr_ref[...]
    loga = _log_sigmoid(x) * (1.0 / GATE_TAU)
    q = q_ref[...] * (DK_A ** -0.5)
    k = k_ref[...]
    v = v_ref[...]
    row = lax.broadcasted_iota(jnp.int32, (C, C), 0)
    col = lax.broadcasted_iota(jnp.int32, (C, C), 1)
    tril = (row >= col).astype(F32)
    b = jnp.dot(tril, loga, precision=HIGHEST, preferred_element_type=F32)
    S = S_ref[...]
    o = jnp.dot(q * jnp.exp(b), S, precision=HIGHEST, preferred_element_type=F32)

    tpos = lax.broadcasted_iota(jnp.int32, (sb, 1), 0)
    outs = []
    for i in range(C // sb):
        r0 = i * sb
        bi, qi, ki, vi = b[r0:r0 + sb], q[r0:r0 + sb], k[r0:r0 + sb], v[r0:r0 + sb]
        oi = o[r0:r0 + sb]
        if i > 0:
            piv = b[r0 - 1:r0]
            qt = qi * jnp.exp(bi - piv)
            kt = k[:r0] * jnp.exp(piv - b[:r0])
            att = lax.dot_general(qt, kt, NT_DIMS, precision=HIGHEST, preferred_element_type=F32)
            oi = oi + jnp.dot(att, v[:r0], precision=HIGHEST, preferred_element_type=F32)
        for s in range(sb):
            dec = jnp.exp(jnp.minimum(bi - bi[s:s + 1], 0.0))
            a_col = jnp.sum(qi * ki[s:s + 1] * dec, axis=-1, keepdims=True)
            a_col = jnp.where(tpos >= s, a_col, 0.0)
            oi = oi + a_col * vi[s:s + 1]
        outs.append(oi)
    o = outs[0] if len(outs) == 1 else jnp.concatenate(outs, axis=0)

    ms = jnp.mean(o * o, axis=-1, keepdims=True)
    on = o * lax.rsqrt(ms + EPS) * gn_ref[...]
    o_ref[...] = (on * _silu(g_ref[...])).astype(o_ref.dtype)

    blast = b[C - 1:C]
    kd = k * jnp.exp(blast - b)
    tot = lax.dot_general(loga, jnp.ones((C, DV_A), F32), TN_DIMS, precision=HIGHEST,
                          preferred_element_type=F32)
    S_new = jnp.exp(tot) * S + lax.dot_general(kd, v, TN_DIMS, precision=HIGHEST,
                                               preferred_element_type=F32)
    S_ref[...] = S_new
    sout_ref[0, 0] = S_new


def gla_mixer(P, LR, wlr, blr, gn, s0, nb, nchunks, C, row_block0, rows_out):
    sb = min(16, C)
    rb = lambda b, h, c: row_block0 + b * nchunks + c
    kv_blk = QA // DV_A
    return pl.pallas_call(
        functools.partial(_gla_kernel, C=C, sb=sb),
        grid=(nb, H_A, nchunks),
        in_specs=[
            pl.BlockSpec((C, DK_A), lambda b, h, c: (rb(b, h, c), h)),
            pl.BlockSpec((C, DK_A), lambda b, h, c: (rb(b, h, c), H_A + h)),
            pl.BlockSpec((C, DV_A), lambda b, h, c: (rb(b, h, c), 2 * kv_blk + h)),
            pl.BlockSpec((C, DV_A), lambda b, h, c: (rb(b, h, c), 2 * kv_blk + H_A + h)),
            pl.BlockSpec((C, LANE), lambda b, h, c: (rb(b, h, c), 0)),
            pl.BlockSpec((LANE, DK_A), lambda b, h, c: (0, h)),
            pl.BlockSpec((1, DK_A), lambda b, h, c: (0, h)),
            pl.BlockSpec((1, DV_A), lambda b, h, c: (0, 0)),
            pl.BlockSpec((1, 1, DK_A, DV_A), lambda b, h, c: (b, h, 0, 0)),
        ],
        out_specs=[
            pl.BlockSpec((C, DV_A), lambda b, h, c: (b * nchunks + c, h)),
            pl.BlockSpec((1, 1, DK_A, DV_A), lambda b, h, c: (b, h, 0, 0)),
        ],
        out_shape=[jax.ShapeDtypeStruct((rows_out, VA), BF16),
                   jax.ShapeDtypeStruct((nb, H_A, DK_A, DV_A), F32)],
        scratch_shapes=[pltpu.VMEM((DK_A, DV_A), F32)],
        compiler_params=_params("parallel", "parallel", "arbitrary"),
        name="gla_mixer",
    )(P, P, P, P, LR, wlr, blr, gn, s0)


def _conv_kernel(ga_ref, gb_ref, halo_ref, cw_ref, cb_ref, lg_ref, lb_ref, o_ref, tail_ref, u_ref, *, R):
    @pl.when(pl.program_id(1) == 0)
    def _():
        u_ref[0:HALO] = halo_ref[0]

    u_ref[HALO:HALO + R] = ga_ref[...] * _sigmoid(gb_ref[...])
    acc = jnp.zeros((R, C_B), F32) + cb_ref[...]
    first = HALO - (CONV_W - 1)
    for w in range(CONV_W):
        acc = acc + u_ref[first + w:first + w + R, :] * cw_ref[w:w + 1, :]
    mu = jnp.mean(acc, axis=-1, keepdims=True)
    d = acc - mu
    var = jnp.mean(d * d, axis=-1, keepdims=True)
    y = d * lax.rsqrt(var + EPS) * lg_ref[...] + lb_ref[...]
    o_ref[...] = _silu(y).astype(o_ref.dtype)
    new_halo = u_ref[R:R + HALO]
    u_ref[0:HALO] = new_halo
    tail_ref[0] = new_halo


def conv_mixer(P, halo, cw, cb, lg, lb, nb, nchunks, R, row_block0, rows_out):
    rb = lambda b, c: row_block0 + b * nchunks + c
    ga_blk = (2 * QA + 2 * VA) // C_B
    return pl.pallas_call(
        functools.partial(_conv_kernel, R=R),
        grid=(nb, nchunks),
        in_specs=[
            pl.BlockSpec((R, C_B), lambda b, c: (rb(b, c), ga_blk)),
            pl.BlockSpec((R, C_B), lambda b, c: (rb(b, c), ga_blk + 1)),
            pl.BlockSpec((1, HALO, C_B), lambda b, c: (b, 0, 0)),
            pl.BlockSpec((HALO, C_B), lambda b, c: (0, 0)),
            pl.BlockSpec((1, C_B), lambda b, c: (0, 0)),
            pl.BlockSpec((1, C_B), lambda b, c: (0, 0)),
            pl.BlockSpec((1, C_B), lambda b, c: (0, 0)),
        ],
        out_specs=[
            pl.BlockSpec((R, C_B), lambda b, c: (b * nchunks + c, 0)),
            pl.BlockSpec((1, HALO, C_B), lambda b, c: (b, 0, 0)),
        ],
        out_shape=[jax.ShapeDtypeStruct((rows_out, C_B), BF16),
                   jax.ShapeDtypeStruct((nb, HALO, C_B), F32)],
        scratch_shapes=[pltpu.VMEM((HALO + R, C_B), F32)],
        compiler_params=_params("parallel", "arbitrary"),
        name="conv_mixer",
    )(P, P, halo, cw, cb, lg, lb)


def _sb_tile(z, R, vt, tri, mask):
    sp = _softplus(z)
    if mask is not None:
        sp = jnp.where(mask, sp, 0.0)
    hi = sp.astype(BF16)
    lo = (sp - hi.astype(F32)).astype(BF16)
    aft = (jnp.dot(hi, tri, preferred_element_type=F32)
           + jnp.dot(lo, tri, preferred_element_type=F32) + R)
    a = jnp.exp(z - sp - aft)
    if mask is not None:
        a = jnp.where(mask, a, 0.0)
    contrib = jnp.dot(a.astype(BF16), vt, preferred_element_type=F32)
    return contrib, aft[:, 0:1] + sp[:, 0:1]


def _sb_prompt_kernel(bias_ref, q_ref, k_ref, v_ref, o_ref, *, L, TQ):
    h = pl.program_id(1)
    bias = bias_ref[h]
    row = lax.broadcasted_iota(jnp.int32, (TQ, TQ), 0)
    col = lax.broadcasted_iota(jnp.int32, (TQ, TQ), 1)
    tri = (row > col).astype(BF16)
    causal = col < row

    for qi in range(L // TQ):
        q0 = qi * TQ
        qt = (q_ref[q0:q0 + TQ, :] * (DH_C ** -0.5)).astype(BF16)

        def tile(k0, R, mask):
            kt = k_ref[pl.ds(k0, TQ), :].astype(BF16)
            vt = v_ref[pl.ds(k0, TQ), :].astype(BF16)
            z = lax.dot_general(qt, kt, NT_DIMS, preferred_element_type=F32) + bias
            return _sb_tile(z, R, vt, tri, mask)

        acc, R = tile(q0, jnp.zeros((TQ, 1), F32), causal)

        def body(j, carry):
            acc, R = carry
            k0 = pl.multiple_of((qi - 1 - j) * TQ, SUBLANE)
            contrib, R = tile(k0, R, None)
            return acc + contrib, R

        if qi > 0:
            acc, R = lax.fori_loop(0, qi, body, (acc, R))
        o_ref[q0:q0 + TQ, :] = acc.astype(o_ref.dtype)


def sb_prompt_attn(QKV, bias, nb, L, TQ):
    return pl.pallas_call(
        functools.partial(_sb_prompt_kernel, L=L, TQ=TQ),
        grid=(nb, H_C),
        in_specs=[
            pl.BlockSpec(memory_space=pltpu.SMEM),
            pl.BlockSpec((L, DH_C), lambda b, h: (b, h)),
            pl.BlockSpec((L, DH_C), lambda b, h: (b, H_C + h)),
            pl.BlockSpec((L, DH_C), lambda b, h: (b, 2 * H_C + h)),
        ],
        out_specs=pl.BlockSpec((L, DH_C), lambda b, h: (b, h)),
        out_shape=jax.ShapeDtypeStruct((nb * L, H_C * DH_C), BF16),
        compiler_params=_params("parallel", "parallel"),
        name="sb_prompt_attn",
    )(bias, QKV, QKV, QKV)


def _sb_sample_kernel(pt_ref, bias_ref, q_ref, kn_ref, vn_ref, kc_ref, vc_ref, o_ref,
                      qbd_ref, R_ref, acc_ref, *, DS, n_pages):
    del pt_ref
    p = pl.program_id(1)
    HQ = H_C * DS
    D = H_C * DH_C

    def bias_row():
        lane = lax.broadcasted_iota(jnp.int32, (1, HQ), 1)
        out = jnp.zeros((1, HQ), F32)
        for h in range(H_C):
            out = jnp.where(lane // DS == h, bias_ref[h], out)
        return out

    @pl.when(p == 0)
    def _():
        qs = q_ref[...] * (DH_C ** -0.5)
        eye = (lax.broadcasted_iota(jnp.int32, (DS, DS), 0)
               == lax.broadcasted_iota(jnp.int32, (DS, DS), 1)).astype(F32)
        qT = lax.dot_general(qs, eye, TN_DIMS, precision=HIGHEST, preferred_element_type=F32)
        rep = jnp.concatenate([qT] * H_C, axis=1)
        r_head = lax.broadcasted_iota(jnp.int32, (D, HQ), 0) // DH_C
        c_head = lax.broadcasted_iota(jnp.int32, (D, HQ), 1) // DS
        qbd = jnp.where(r_head == c_head, rep, 0.0).astype(BF16)
        qbd_ref[...] = qbd
        z = jnp.dot(kn_ref[...].astype(BF16), qbd, preferred_element_type=F32) + bias_row()
        jrow = lax.broadcasted_iota(jnp.int32, (DS, HQ), 0)
        tq = lax.broadcasted_iota(jnp.int32, (DS, HQ), 1) % DS
        mask = jrow < tq
        sp = jnp.where(mask, _softplus(z), 0.0)
        triu = (lax.broadcasted_iota(jnp.int32, (DS, DS), 1)
                > lax.broadcasted_iota(jnp.int32, (DS, DS), 0)).astype(F32)
        aft = jnp.dot(triu, sp, precision=HIGHEST, preferred_element_type=F32)
        a = jnp.where(mask, jnp.exp(z - sp - aft), 0.0)
        acc_ref[...] = lax.dot_general(a.astype(BF16), vn_ref[...].astype(BF16), TN_DIMS,
                                       preferred_element_type=F32)
        R_ref[...] = jnp.sum(sp, axis=0, keepdims=True)

    kp = kc_ref[0].astype(BF16)
    vp = vc_ref[0].astype(BF16)
    z = jnp.dot(kp, qbd_ref[...], preferred_element_type=F32) + bias_row()
    sp = _softplus(z)
    hi = sp.astype(BF16)
    lo = (sp - hi.astype(F32)).astype(BF16)
    triu = (lax.broadcasted_iota(jnp.int32, (PAGE_SIZE, PAGE_SIZE), 1)
            > lax.broadcasted_iota(jnp.int32, (PAGE_SIZE, PAGE_SIZE), 0)).astype(BF16)
    aft = (jnp.dot(triu, hi, preferred_element_type=F32)
           + jnp.dot(triu, lo, preferred_element_type=F32) + R_ref[...])
    a = jnp.exp(z - sp - aft)
    acc_ref[...] += lax.dot_general(a.astype(BF16), vp, TN_DIMS, preferred_element_type=F32)
    R_ref[...] = aft[0:1, :] + sp[0:1, :]

    @pl.when(p == n_pages - 1)
    def _():
        for h in range(H_C):
            o_ref[:, h * DH_C:(h + 1) * DH_C] = acc_ref[h * DS:(h + 1) * DS,
                                                        h * DH_C:(h + 1) * DH_C].astype(o_ref.dtype)


def sb_sample_attn(QKV, bias, cache_k, cache_v, page_table, DB, DS, row_block0):
    n_pool = cache_k.shape[0]
    n_pages = page_table.shape[1]
    D = H_C * DH_C
    kc = cache_k.reshape(n_pool, PAGE_SIZE, D)
    vc = cache_v.reshape(n_pool, PAGE_SIZE, D)
    page = lambda b, p, pt: (pt[b, n_pages - 1 - p], 0, 0)
    grid_spec = pltpu.PrefetchScalarGridSpec(
        num_scalar_prefetch=1,
        grid=(DB, n_pages),
        in_specs=[
            pl.BlockSpec(memory_space=pltpu.SMEM),
            pl.BlockSpec((DS, D), lambda b, p, pt: (row_block0 + b, 0)),
            pl.BlockSpec((DS, D), lambda b, p, pt: (row_block0 + b, 1)),
            pl.BlockSpec((DS, D), lambda b, p, pt: (row_block0 + b, 2)),
            pl.BlockSpec((1, PAGE_SIZE, D), page),
            pl.BlockSpec((1, PAGE_SIZE, D), page),
        ],
        out_specs=pl.BlockSpec((DS, D), lambda b, p, pt: (b, 0)),
        scratch_shapes=[pltpu.VMEM((D, H_C * DS), BF16),
                        pltpu.VMEM((1, H_C * DS), F32),
                        pltpu.VMEM((H_C * DS, D), F32)],
    )
    return pl.pallas_call(
        functools.partial(_sb_sample_kernel, DS=DS, n_pages=n_pages),
        grid_spec=grid_spec,
        out_shape=jax.ShapeDtypeStruct((DB * DS, D), BF16),
        compiler_params=_params("parallel", "arbitrary"),
        name="sb_sample_attn",
    )(page_table, bias, QKV, QKV, QKV, kc, vc)


def _top_desc(x, n):
    rows = lax.broadcasted_iota(jnp.int32, x.shape, 0)
    big = jnp.int32(x.shape[0])
    vals = []
    for _ in range(n):
        m = jnp.max(x, axis=0, keepdims=True)
        first = jnp.min(jnp.where(x == m, rows, big), axis=0, keepdims=True)
        x = jnp.where(rows == first, -jnp.inf, x)
        vals.append(m)
    return vals


def _peer_route_kernel(q_ref, keys_ref, s_ref, st_ref, *, TR):
    K = PEER_TOPK
    half = PEER_DQ // 2
    for h in range(PEER_HEADS):
        tops = []
        for c in range(2):
            hc = 2 * h + c
            qh = q_ref[:, hc * half:(hc + 1) * half].astype(BF16)
            s = lax.dot_general(keys_ref[hc], qh, NT_DIMS, preferred_element_type=F32)
            s_ref[hc] = s
            tops.append(_top_desc(s, K))
        v1, v2 = tops
        cand = []
        for a in range(K):
            nb = K // (a + 1)
            cand.extend(v1[a] + v2[b] for b in range(nb))
        ncand = len(cand)
        pad = (-ncand) % SUBLANE
        cand = jnp.concatenate(cand + [jnp.full((pad, TR), -jnp.inf, F32)], axis=0)
        thr = _top_desc(cand, K)[-1]
        m = v1[0] + v2[0]
        zsum = jnp.sum(jnp.where(cand >= thr, jnp.exp(cand - m), 0.0), axis=0, keepdims=True)
        st_ref[h] = jnp.concatenate(
            [thr, v1[0], v2[0], 1.0 / zsum, jnp.zeros((4, TR), F32)], axis=0)


def peer_route(q, keys_bf16, TR):
    T = q.shape[0]
    return pl.pallas_call(
        functools.partial(_peer_route_kernel, TR=TR),
        grid=(T // TR,),
        in_specs=[pl.BlockSpec((TR, PEER_HEADS * PEER_DQ), lambda i: (i, 0)),
                  pl.BlockSpec((2 * PEER_HEADS, PEER_NKEYS, PEER_DQ // 2), lambda i: (0, 0, 0))],
        out_specs=[pl.BlockSpec((2 * PEER_HEADS, PEER_NKEYS, TR), lambda i: (0, 0, i)),
                   pl.BlockSpec((PEER_HEADS, 8, TR), lambda i: (0, 0, i))],
        out_shape=[jax.ShapeDtypeStruct((2 * PEER_HEADS, PEER_NKEYS, T), F32),
                   jax.ShapeDtypeStruct((PEER_HEADS, 8, T), F32)],
        compiler_params=_params("parallel"),
        name="peer_route",
    )(q, keys_bf16)


def _peer_dense_kernel(h_ref, u_ref, vt_ref, s_ref, st_ref, o_ref, e_ref, *, R, TB):
    j = pl.program_id(1)

    @pl.when(j == 0)
    def _():
        o_ref[...] = jnp.zeros_like(o_ref)
        for h in range(PEER_HEADS):
            e_ref[2 * h] = jnp.exp(s_ref[2 * h] - st_ref[h, 1:2, :])
            e_ref[2 * h + 1] = jnp.exp(s_ref[2 * h + 1] - st_ref[h, 2:3, :]) * st_ref[h, 3:4, :]

    a_all = lax.dot_general(u_ref[...], h_ref[...], NT_DIMS, preferred_element_type=F32)
    parts = []
    for r in range(R):
        i1 = j * R + r
        a = a_all[r * PEER_NKEYS:(r + 1) * PEER_NKEYS]
        gate = jnp.zeros((PEER_NKEYS, TB), F32)
        for h in range(PEER_HEADS):
            s1 = s_ref[2 * h, pl.ds(i1, 1), :]
            e1 = e_ref[2 * h, pl.ds(i1, 1), :]
            sel = (s1 + s_ref[2 * h + 1]) >= st_ref[h, 0:1, :]
            gate = gate + jnp.where(sel, e1 * e_ref[2 * h + 1], 0.0)
        parts.append((gate * _gelu_tanh(a)).astype(BF16))
    pt = parts[0] if R == 1 else jnp.concatenate(parts, axis=0)
    o_ref[...] += jnp.dot(vt_ref[...], pt, preferred_element_type=F32)


def peer_dense(h, u_bf16, vt_bf16, s, st, TB, R):
    T, D = h.shape
    n_i1 = PEER_NKEYS
    return pl.pallas_call(
        functools.partial(_peer_dense_kernel, R=R, TB=TB),
        grid=(T // TB, n_i1 // R),
        in_specs=[pl.BlockSpec((TB, D), lambda i, j: (i, 0)),
                  pl.BlockSpec((R * PEER_NKEYS, D), lambda i, j: (j, 0)),
                  pl.BlockSpec((D, R * PEER_NKEYS), lambda i, j: (0, j)),
                  pl.BlockSpec((2 * PEER_HEADS, PEER_NKEYS, TB), lambda i, j: (0, 0, i)),
                  pl.BlockSpec((PEER_HEADS, 8, TB), lambda i, j: (0, 0, i))],
        out_specs=pl.BlockSpec((D, TB), lambda i, j: (0, i)),
        out_shape=jax.ShapeDtypeStruct((D, T), F32),
        scratch_shapes=[pltpu.VMEM((2 * PEER_HEADS, PEER_NKEYS, TB), F32)],
        compiler_params=_params("parallel", "arbitrary"),
        name="peer_dense",
    )(h, u_bf16, vt_bf16, s, st)


def peer_layer(x, g, wq, keys, u, v, TB):
    q, hn = rms_matmul(x, g, wq.astype(BF16), TB, 512, emit_h=True)
    keys_b = keys.reshape(2 * PEER_HEADS, PEER_NKEYS, PEER_DQ // 2).astype(BF16)
    s, st = peer_route(q, keys_b, 256)
    yt = peer_dense(hn, u.astype(BF16), v.T.astype(BF16), s, st, TB, 4)
    return transpose_add(x, yt, TB)


def _pick_tile(n, cap, mult):
    best = mult
    for t in range(mult, cap + 1, mult):
        if n % t == 0:
            best = t
    return best


def kernel(x_prompt, x_sample, state_gla, state_conv, cache_k, cache_v, page_table, meta_tokens, norm_mix, norm_ffn, norm_final, w_in_even, w_gate_lr, b_gate_lr, gla_norm, conv_w, conv_b, conv_norm_g, conv_norm_b, w_out_even, w_qkv_odd, w_out_odd, sb_bias, peer_wq, peer_keys, peer_u, peer_v):
    B, SEQ, D = x_prompt.shape
    DB, DS, _ = x_sample.shape
    depth = norm_mix.shape[0]
    L = N_META + SEQ
    Tp, Ts = B * L, DB * DS
    T = Tp + Ts
    TB = 768 if T >= 768 else 256
    Tpad = -(-T // TB) * TB
    GC = 48 if L % 48 == 0 else 16
    assert L % GC == 0 and Tp % DS == 0 and DS % SUBLANE == 0 and DS <= 16
    TQ = _pick_tile(L, 384, SUBLANE)

    xp = jnp.concatenate([jnp.broadcast_to(meta_tokens[None], (B, N_META, D)), x_prompt], axis=1)
    X = jnp.concatenate([xp.reshape(Tp, D), x_sample.reshape(Ts, D), jnp.zeros((Tpad - T, D), F32)], axis=0)

    gla_p, gla_s, conv_p, conv_s, k_p, v_p, k_s, v_s = [], [], [], [], [], [], [], []
    for layer in range(depth):
        i = layer // 2
        if layer % 2 == 0:
            w = w_in_even[i]
            n_main = 2 * QA + 2 * VA
            w_main = jnp.concatenate([w[:, :n_main], w[:, n_main + LOWRANK_A:]], axis=1).astype(BF16)
            w_lr = jnp.pad(w[:, n_main:n_main + LOWRANK_A], ((0, 0), (0, LANE - LOWRANK_A))).astype(BF16)
            P = rms_matmul(X, norm_mix[layer], w_main, TB, 512)
            LR = rms_matmul(X, norm_mix[layer], w_lr, TB, LANE)
            wlr = jnp.pad(w_gate_lr[i], ((0, LANE - LOWRANK_A), (0, 0))).astype(BF16)
            blr = b_gate_lr[i].reshape(1, QA)
            gn = gla_norm[i].reshape(1, DV_A)
            oa_p, sp_ = gla_mixer(P, LR, wlr, blr, gn, jnp.zeros((B, H_A, DK_A, DV_A), F32),
                                  B, L // GC, GC, 0, Tp)
            oa_s, ss_ = gla_mixer(P, LR, wlr, blr, gn, state_gla[i], DB, 1, DS, Tp // DS, Ts)
            cw = jnp.pad(conv_w[i], ((0, HALO - CONV_W), (0, 0)))
            cvec = [t.reshape(1, C_B) for t in (conv_b[i], conv_norm_g[i], conv_norm_b[i])]
            ob_p, tail_p = conv_mixer(P, jnp.zeros((B, HALO, C_B), F32), cw, *cvec, B, L // GC, GC, 0, Tp)
            halo_s = jnp.pad(state_conv[i], ((0, 0), (HALO - (CONV_W - 1), 0), (0, 0)))
            ob_s, tail_s = conv_mixer(P, halo_s, cw, *cvec, DB, 1, DS, Tp // DS, Ts)
            mix = jnp.concatenate([jnp.concatenate([oa_p, ob_p], axis=1),
                                   jnp.concatenate([oa_s, ob_s], axis=1),
                                   jnp.zeros((Tpad - T, VA + C_B), BF16)], axis=0)
            X = matmul_res(mix, w_out_even[i].astype(BF16), X, TB, 512)
            gla_p.append(sp_)
            gla_s.append(ss_)
            conv_p.append(tail_p[:, HALO - (CONV_W - 1):])
            conv_s.append(tail_s[:, HALO - (CONV_W - 1):])
        else:
            QKV = rms_matmul(X, norm_mix[layer], w_qkv_odd[i].astype(BF16), TB, 512)
            HD = H_C * DH_C
            o_p = sb_prompt_attn(QKV, sb_bias[i], B, L, TQ)
            o_s = sb_sample_attn(QKV, sb_bias[i], cache_k[i], cache_v[i], page_table, DB, DS, Tp // DS)
            att = jnp.concatenate([o_p, o_s, jnp.zeros((Tpad - T, HD), BF16)], axis=0)
            X = matmul_res(att, w_out_odd[i].astype(BF16), X, TB, 512)
            k_p.append(QKV[:Tp, HD:2 * HD].reshape(B, L, H_C, DH_C))
            v_p.append(QKV[:Tp, 2 * HD:].reshape(B, L, H_C, DH_C))
            k_s.append(QKV[Tp:T, HD:2 * HD].reshape(DB, DS, H_C, DH_C))
            v_s.append(QKV[Tp:T, 2 * HD:].reshape(DB, DS, H_C, DH_C))
        X = peer_layer(X, norm_ffn[layer], peer_wq[layer], peer_keys[layer], peer_u[layer], peer_v[layer], TB)

    Y = rmsnorm_rows(X, norm_final, TB)
    y_prompt = Y[:Tp].reshape(B, L, D)[:, N_META:]
    y_sample = Y[Tp:T].reshape(DB, DS, D)
    return (y_prompt, y_sample, jnp.stack(gla_p), jnp.stack(gla_s), jnp.stack(conv_p), jnp.stack(conv_s),
            jnp.stack(k_p), jnp.stack(v_p), jnp.stack(k_s), jnp.stack(v_s))
```

```python
import functools

import jax
import jax.numpy as jnp
from jax import lax
from jax.experimental import pallas as pl
from jax.experimental.pallas import tpu as pltpu

F32 = jnp.float32
BF16 = jnp.bfloat16
HIGHEST = lax.Precision.HIGHEST

N_META = 16
EPS = 1e-6
H_A = 4
DK_A = 128
DV_A = 256
LOWRANK_A = 16
GATE_TAU = 16.0
C_B = 1024
CONV_W = 31
H_C = 16
DH_C = 128
PAGE_SIZE = 128
PEER_HEADS = 8
PEER_NKEYS = 128
PEER_DQ = 256
PEER_TOPK = 16
QA = H_A * DK_A
VA = H_A * DV_A

LANE = 128
SUBLANE = 8
HALO = 32
VMEM_LIMIT = 56 * 1024 * 1024

NT_DIMS = (((1,), (1,)), ((), ()))
TN_DIMS = (((0,), (0,)), ((), ()))


def _params(*sem):
    return pltpu.CompilerParams(dimension_semantics=sem, vmem_limit_bytes=VMEM_LIMIT)


def _softplus(x):
    return jnp.maximum(x, 0.0) + jnp.log1p(jnp.exp(-jnp.abs(x)))


def _log_sigmoid(x):
    return -_softplus(-x)


def _sigmoid(x):
    return 1.0 / (1.0 + jnp.exp(-x))


def _silu(x):
    return x * _sigmoid(x)


def _gelu_tanh(x):
    c = 0.7978845608028654
    return 0.5 * x * (1.0 + jnp.tanh(c * (x + 0.044715 * (x * x * x))))


def _rms_matmul_kernel(x_ref, g_ref, w_ref, o_ref, *rest, emit_h):
    h_ref = rest[-1]

    @pl.when(pl.program_id(1) == 0)
    def _():
        x = x_ref[...]
        ms = jnp.mean(x * x, axis=-1, keepdims=True)
        h = (x * lax.rsqrt(ms + EPS) * g_ref[...]).astype(BF16)
        h_ref[...] = h
        if emit_h:
            rest[0][...] = h

    o_ref[...] = jnp.dot(h_ref[...], w_ref[...], preferred_element_type=F32)


def rms_matmul(x, g, w, tm, tn, emit_h=False):
    T, D = x.shape
    N = w.shape[1]
    out_shape = [jax.ShapeDtypeStruct((T, N), F32)]
    out_specs = [pl.BlockSpec((tm, tn), lambda i, j: (i, j))]
    if emit_h:
        out_shape.append(jax.ShapeDtypeStruct((T, D), BF16))
        out_specs.append(pl.BlockSpec((tm, D), lambda i, j: (i, 0)))
    res = pl.pallas_call(
        functools.partial(_rms_matmul_kernel, emit_h=emit_h),
        grid=(T // tm, N // tn),
        in_specs=[pl.BlockSpec((tm, D), lambda i, j: (i, 0)),
                  pl.BlockSpec((1, D), lambda i, j: (0, 0)),
                  pl.BlockSpec((D, tn), lambda i, j: (0, j))],
        out_specs=out_specs,
        out_shape=out_shape,
        scratch_shapes=[pltpu.VMEM((tm, D), BF16)],
        compiler_params=_params("parallel", "arbitrary"),
        name="rms_matmul",
    )(x, g.reshape(1, D), w)
    return res if emit_h else res[0]


def _matmul_res_kernel(a_ref, w_ref, r_ref, o_ref):
    o_ref[...] = r_ref[...] + jnp.dot(a_ref[...], w_ref[...], preferred_element_type=F32)


def matmul_res(a, w, res, tm, tn):
    T, K = a.shape
    N = w.shape[1]
    return pl.pallas_call(
        _matmul_res_kernel,
        grid=(T // tm, N // tn),
        in_specs=[pl.BlockSpec((tm, K), lambda i, j: (i, 0)),
                  pl.BlockSpec((K, tn), lambda i, j: (0, j)),
                  pl.BlockSpec((tm, tn), lambda i, j: (i, j))],
        out_specs=pl.BlockSpec((tm, tn), lambda i, j: (i, j)),
        out_shape=jax.ShapeDtypeStruct((T, N), F32),
        compiler_params=_params("parallel", "arbitrary"),
        name="matmul_res",
    )(a, w, res)


def _rmsnorm_kernel(x_ref, g_ref, o_ref):
    x = x_ref[...]
    ms = jnp.mean(x * x, axis=-1, keepdims=True)
    o_ref[...] = x * lax.rsqrt(ms + EPS) * g_ref[...]


def rmsnorm_rows(x, g, tm):
    T, D = x.shape
    return pl.pallas_call(
        _rmsnorm_kernel,
        grid=(T // tm,),
        in_specs=[pl.BlockSpec((tm, D), lambda i: (i, 0)),
                  pl.BlockSpec((1, D), lambda i: (0, 0))],
        out_specs=pl.BlockSpec((tm, D), lambda i: (i, 0)),
        out_shape=jax.ShapeDtypeStruct((T, D), F32),
        compiler_params=_params("parallel"),
        name="rmsnorm_final",
    )(x, g.reshape(1, D))


def _transpose_add_kernel(x_ref, yt_ref, o_ref):
    o_ref[...] = x_ref[...] + yt_ref[...].T


def transpose_add(x, yt, tb):
    T, D = x.shape
    return pl.pallas_call(
        _transpose_add_kernel,
        grid=(T // tb,),
        in_specs=[pl.BlockSpec((tb, D), lambda i: (i, 0)),
                  pl.BlockSpec((D, tb), lambda i: (0, i))],
        out_specs=pl.BlockSpec((tb, D), lambda i: (i, 0)),
        out_shape=jax.ShapeDtypeStruct((T, D), F32),
        compiler_params=_params("parallel"),
        name="transpose_add",
    )(x, yt)


def _gla_kernel(q_ref, k_ref, v_ref, g_ref, lr_ref, wlr_ref, blr_ref, gn_ref, s0_ref,
                o_ref, sout_ref, S_ref, *, C, sb):
    c = pl.program_id(1)

    @pl.when(c == 0)
    def _():
        S_ref[...] = s0_ref[0]

    x = jnp.dot(lr_ref[...].astype(BF16), wlr_ref[...], preferred_element_type=F32) + blr_ref[...]
    loga_all = _log_sigmoid(x) * (1.0 / GATE_TAU)
    row = lax.broadcasted_iota(jnp.int32, (C, C), 0)
    col = lax.broadcasted_iota(jnp.int32, (C, C), 1)
    tril = (row >= col).astype(F32)
    b_all = jnp.dot(tril, loga_all, precision=HIGHEST, preferred_element_type=F32)
    for h in range(H_A):
        kk = slice(h * DK_A, (h + 1) * DK_A)
        vv = slice(h * DV_A, (h + 1) * DV_A)
        o, S_new = _gla_head(q_ref[:, kk] * (DK_A ** -0.5), k_ref[:, kk], v_ref[:, vv], b_all[:, kk],
                             loga_all[:, kk], S_ref[h], C, sb)
        ms = jnp.mean(o * o, axis=-1, keepdims=True)
        on = o * lax.rsqrt(ms + EPS) * gn_ref[...]
        o_ref[:, vv] = (on * _silu(g_ref[:, vv])).astype(o_ref.dtype)
        S_ref[h] = S_new
        sout_ref[0, h] = S_new


def _gla_head(q, k, v, b, loga, S, C, sb):
    o = jnp.dot(q * jnp.exp(b), S, precision=HIGHEST, preferred_element_type=F32)

    tpos = lax.broadcasted_iota(jnp.int32, (sb, 1), 0)
    outs = []
    for i in range(C // sb):
        r0 = i * sb
        bi, qi, ki, vi = b[r0:r0 + sb], q[r0:r0 + sb], k[r0:r0 + sb], v[r0:r0 + sb]
        oi = o[r0:r0 + sb]
        if i > 0:
            piv = b[r0 - 1:r0]
            qt = qi * jnp.exp(bi - piv)
            kt = k[:r0] * jnp.exp(piv - b[:r0])
            att = lax.dot_general(qt, kt, NT_DIMS, precision=HIGHEST, preferred_element_type=F32)
            oi = oi + jnp.dot(att, v[:r0], precision=HIGHEST, preferred_element_type=F32)
        for s in range(sb):
            dec = jnp.exp(jnp.minimum(bi - bi[s:s + 1], 0.0))
            a_col = jnp.sum(qi * ki[s:s + 1] * dec, axis=-1, keepdims=True)
            a_col = jnp.where(tpos >= s, a_col, 0.0)
            oi = oi + a_col * vi[s:s + 1]
        outs.append(oi)
    o = outs[0] if len(outs) == 1 else jnp.concatenate(outs, axis=0)

    blast = b[C - 1:C]
    kd = k * jnp.exp(blast - b)
    tot = lax.dot_general(loga, jnp.ones((C, DV_A), F32), TN_DIMS, precision=HIGHEST,
                          preferred_element_type=F32)
    S_new = jnp.exp(tot) * S + lax.dot_general(kd, v, TN_DIMS, precision=HIGHEST,
                                               preferred_element_type=F32)
    return o, S_new


def gla_mixer(P, LR, wlr, blr, gn, s0, nb, nchunks, C, row_block0, rows_out):
    sb = min(16, C)
    rb = lambda b, c: row_block0 + b * nchunks + c
    return pl.pallas_call(
        functools.partial(_gla_kernel, C=C, sb=sb),
        grid=(nb, nchunks),
        in_specs=[
            pl.BlockSpec((C, QA), lambda b, c: (rb(b, c), 0)),
            pl.BlockSpec((C, QA), lambda b, c: (rb(b, c), 1)),
            pl.BlockSpec((C, VA), lambda b, c: (rb(b, c), 2 * QA // VA)),
            pl.BlockSpec((C, VA), lambda b, c: (rb(b, c), 2 * QA // VA + 1)),
            pl.BlockSpec((C, LANE), lambda b, c: (rb(b, c), 0)),
            pl.BlockSpec((LANE, QA), lambda b, c: (0, 0)),
            pl.BlockSpec((1, QA), lambda b, c: (0, 0)),
            pl.BlockSpec((1, DV_A), lambda b, c: (0, 0)),
            pl.BlockSpec((1, H_A, DK_A, DV_A), lambda b, c: (b, 0, 0, 0)),
        ],
        out_specs=[
            pl.BlockSpec((C, VA), lambda b, c: (b * nchunks + c, 0)),
            pl.BlockSpec((1, H_A, DK_A, DV_A), lambda b, c: (b, 0, 0, 0)),
        ],
        out_shape=[jax.ShapeDtypeStruct((rows_out, VA), BF16),
                   jax.ShapeDtypeStruct((nb, H_A, DK_A, DV_A), F32)],
        scratch_shapes=[pltpu.VMEM((H_A, DK_A, DV_A), F32)],
        compiler_params=_params("parallel", "arbitrary"),
        name="gla_mixer",
    )(P, P, P, P, LR, wlr, blr, gn, s0)


def _conv_kernel(ga_ref, gb_ref, halo_ref, cw_ref, cb_ref, lg_ref, lb_ref, o_ref, tail_ref, u_ref, *, R):
    @pl.when(pl.program_id(1) == 0)
    def _():
        u_ref[0:HALO] = halo_ref[0]

    u_ref[HALO:HALO + R] = ga_ref[...] * _sigmoid(gb_ref[...])
    acc = jnp.zeros((R, C_B), F32) + cb_ref[...]
    first = HALO - (CONV_W - 1)
    for w in range(CONV_W):
        acc = acc + u_ref[first + w:first + w + R, :] * cw_ref[w:w + 1, :]
    mu = jnp.mean(acc, axis=-1, keepdims=True)
    d = acc - mu
    var = jnp.mean(d * d, axis=-1, keepdims=True)
    y = d * lax.rsqrt(var + EPS) * lg_ref[...] + lb_ref[...]
    o_ref[...] = _silu(y).astype(o_ref.dtype)
    new_halo = u_ref[R:R + HALO]
    u_ref[0:HALO] = new_halo
    tail_ref[0] = new_halo


def conv_mixer(P, halo, cw, cb, lg, lb, nb, nchunks, R, row_block0, rows_out):
    rb = lambda b, c: row_block0 + b * nchunks + c
    ga_blk = (2 * QA + 2 * VA) // C_B
    return pl.pallas_call(
        functools.partial(_conv_kernel, R=R),
        grid=(nb, nchunks),
        in_specs=[
            pl.BlockSpec((R, C_B), lambda b, c: (rb(b, c), ga_blk)),
            pl.BlockSpec((R, C_B), lambda b, c: (rb(b, c), ga_blk + 1)),
            pl.BlockSpec((1, HALO, C_B), lambda b, c: (b, 0, 0)),
            pl.BlockSpec((HALO, C_B), lambda b, c: (0, 0)),
            pl.BlockSpec((1, C_B), lambda b, c: (0, 0)),
            pl.BlockSpec((1, C_B), lambda b, c: (0, 0)),
            pl.BlockSpec((1, C_B), lambda b, c: (0, 0)),
        ],
        out_specs=[
            pl.BlockSpec((R, C_B), lambda b, c: (b * nchunks + c, 0)),
            pl.BlockSpec((1, HALO, C_B), lambda b, c: (b, 0, 0)),
        ],
        out_shape=[jax.ShapeDtypeStruct((rows_out, C_B), BF16),
                   jax.ShapeDtypeStruct((nb, HALO, C_B), F32)],
        scratch_shapes=[pltpu.VMEM((HALO + R, C_B), F32)],
        compiler_params=_params("parallel", "arbitrary"),
        name="conv_mixer",
    )(P, P, halo, cw, cb, lg, lb)


def _sb_tile(z, R, vt, tri, mask):
    sp = _softplus(z)
    if mask is not None:
        sp = jnp.where(mask, sp, 0.0)
    hi = sp.astype(BF16)
    lo = (sp - hi.astype(F32)).astype(BF16)
    aft = (jnp.dot(hi, tri, preferred_element_type=F32)
           + jnp.dot(lo, tri, preferred_element_type=F32) + R)
    a = jnp.exp(z - sp - aft)
    if mask is not None:
        a = jnp.where(mask, a, 0.0)
    contrib = jnp.dot(a.astype(BF16), vt, preferred_element_type=F32)
    return contrib, aft[:, 0:1] + sp[:, 0:1]


SB_HEADS_PER_STEP = 2


def _sb_prompt_kernel(bias_ref, q_ref, k_ref, v_ref, o_ref, *, L, TQ):
    M = N_META
    HPS = SB_HEADS_PER_STEP
    heads = range(HPS)
    hcol = [slice(i * DH_C, (i + 1) * DH_C) for i in heads]
    bias = [bias_ref[pl.program_id(1) * HPS + i] for i in heads]

    def tri_and_causal(n):
        row = lax.broadcasted_iota(jnp.int32, (n, n), 0)
        col = lax.broadcasted_iota(jnp.int32, (n, n), 1)
        return (row > col).astype(BF16), col < row

    def qrows(r0, n):
        return [(q_ref[r0:r0 + n, hcol[i]] * (DH_C ** -0.5)).astype(BF16) for i in heads]

    def tiles(qts, k0, n, Rs, tri, mask):
        out = []
        for i in heads:
            kt = k_ref[pl.ds(k0, n), hcol[i]].astype(BF16)
            vt = v_ref[pl.ds(k0, n), hcol[i]].astype(BF16)
            z = lax.dot_general(qts[i], kt, NT_DIMS, preferred_element_type=F32) + bias[i]
            out.append(_sb_tile(z, Rs[i], vt, tri, mask))
        return [o[0] for o in out], [o[1] for o in out]

    tri_m, causal_m = tri_and_causal(M)
    tri_q, causal_q = tri_and_causal(TQ)

    accs, _ = tiles(qrows(0, M), 0, M, [jnp.zeros((M, 1), F32)] * HPS, tri_m, causal_m)
    for i in heads:
        o_ref[0:M, hcol[i]] = accs[i].astype(o_ref.dtype)

    for qi in range((L - M) // TQ):
        q0 = M + qi * TQ
        qts = qrows(q0, TQ)
        accs, Rs = tiles(qts, q0, TQ, [jnp.zeros((TQ, 1), F32)] * HPS, tri_q, causal_q)

        def body(j, carry, qts=qts, qi=qi):
            accs, Rs = carry
            k0 = pl.multiple_of(M + (qi - 1 - j) * TQ, SUBLANE)
            contribs, Rs = tiles(qts, k0, TQ, Rs, tri_q, None)
            return [a + c for a, c in zip(accs, contribs)], Rs

        if qi > 0:
            accs, Rs = lax.fori_loop(0, qi, body, (accs, Rs))
        contribs, _ = tiles(qts, 0, M, Rs, tri_m, None)
        for i in heads:
            o_ref[q0:q0 + TQ, hcol[i]] = (accs[i] + contribs[i]).astype(o_ref.dtype)


def sb_prompt_attn(QKV, bias, nb, L, TQ):
    HPS = SB_HEADS_PER_STEP
    W = HPS * DH_C
    nh = H_C // HPS
    return pl.pallas_call(
        functools.partial(_sb_prompt_kernel, L=L, TQ=TQ),
        grid=(nb, nh),
        in_specs=[
            pl.BlockSpec(memory_space=pltpu.SMEM),
            pl.BlockSpec((L, W), lambda b, h: (b, h)),
            pl.BlockSpec((L, W), lambda b, h: (b, nh + h)),
            pl.BlockSpec((L, W), lambda b, h: (b, 2 * nh + h)),
        ],
        out_specs=pl.BlockSpec((L, W), lambda b, h: (b, h)),
        out_shape=jax.ShapeDtypeStruct((nb * L, H_C * DH_C), BF16),
        compiler_params=_params("parallel", "parallel"),
        name="sb_prompt_attn",
    )(bias, QKV, QKV, QKV)


def _sb_sample_kernel(pt_ref, bias_ref, q_ref, kn_ref, vn_ref, *rest, DS, n_steps, PP):
    del pt_ref
    NHB = H_C // SUBLANE
    kc_refs, vc_refs = rest[:PP * NHB], rest[PP * NHB:2 * PP * NHB]
    o_ref, qbd_ref, R_ref, acc_ref = rest[2 * PP * NHB:]
    p = pl.program_id(1)
    HQ = H_C * DS
    D = H_C * DH_C
    NK = PP * PAGE_SIZE

    def bias_row():
        lane = lax.broadcasted_iota(jnp.int32, (1, HQ), 1)
        out = jnp.zeros((1, HQ), F32)
        for h in range(H_C):
            out = jnp.where(lane // DS == h, bias_ref[h], out)
        return out

    def page_rows(refs):
        return jnp.concatenate([r[0, :, hh, :] for r in refs for hh in range(SUBLANE)],
                               axis=1).astype(BF16)

    @pl.when(p == 0)
    def _():
        qs = q_ref[...] * (DH_C ** -0.5)
        eye = (lax.broadcasted_iota(jnp.int32, (DS, DS), 0)
               == lax.broadcasted_iota(jnp.int32, (DS, DS), 1)).astype(F32)
        qT = lax.dot_general(qs, eye, TN_DIMS, precision=HIGHEST, preferred_element_type=F32)
        rep = jnp.concatenate([qT] * H_C, axis=1)
        r_head = lax.broadcasted_iota(jnp.int32, (D, HQ), 0) // DH_C
        c_head = lax.broadcasted_iota(jnp.int32, (D, HQ), 1) // DS
        qbd = jnp.where(r_head == c_head, rep, 0.0).astype(BF16)
        qbd_ref[...] = qbd
        z = jnp.dot(kn_ref[...].astype(BF16), qbd, preferred_element_type=F32) + bias_row()
        jrow = lax.broadcasted_iota(jnp.int32, (DS, HQ), 0)
        tq = lax.broadcasted_iota(jnp.int32, (DS, HQ), 1) % DS
        mask = jrow < tq
        sp = jnp.where(mask, _softplus(z), 0.0)
        triu = (lax.broadcasted_iota(jnp.int32, (DS, DS), 1)
                > lax.broadcasted_iota(jnp.int32, (DS, DS), 0)).astype(F32)
        aft = jnp.dot(triu, sp, precision=HIGHEST, preferred_element_type=F32)
        a = jnp.where(mask, jnp.exp(z - sp - aft), 0.0)
        acc_ref[...] = lax.dot_general(a.astype(BF16), vn_ref[...].astype(BF16), TN_DIMS,
                                       preferred_element_type=F32)
        R_ref[...] = jnp.sum(sp, axis=0, keepdims=True)

    kp = jnp.concatenate([page_rows(kc_refs[i * NHB:(i + 1) * NHB]) for i in range(PP)], axis=0)
    vp = jnp.concatenate([page_rows(vc_refs[i * NHB:(i + 1) * NHB]) for i in range(PP)], axis=0)
    z = jnp.dot(kp, qbd_ref[...], preferred_element_type=F32) + bias_row()
    sp = _softplus(z)
    hi = sp.astype(BF16)
    lo = (sp - hi.astype(F32)).astype(BF16)
    triu = (lax.broadcasted_iota(jnp.int32, (NK, NK), 1)
            > lax.broadcasted_iota(jnp.int32, (NK, NK), 0)).astype(BF16)
    aft = (jnp.dot(triu, hi, preferred_element_type=F32)
           + jnp.dot(triu, lo, preferred_element_type=F32) + R_ref[...])
    a = jnp.exp(z - sp - aft)
    acc_ref[...] += lax.dot_general(a.astype(BF16), vp, TN_DIMS, preferred_element_type=F32)
    R_ref[...] = aft[0:1, :] + sp[0:1, :]

    @pl.when(p == n_steps - 1)
    def _():
        for h in range(H_C):
            o_ref[:, h * DH_C:(h + 1) * DH_C] = acc_ref[h * DS:(h + 1) * DS,
                                                        h * DH_C:(h + 1) * DH_C].astype(o_ref.dtype)


def sb_sample_attn(QKV, bias, cache_k, cache_v, layer, page_table, DB, DS, row_block0):
    n_layers, n_pool = cache_k.shape[:2]
    n_pages = page_table.shape[1]
    PP = 2 if n_pages % 2 == 0 else 1
    NHB = H_C // SUBLANE
    n_steps = n_pages // PP
    D = H_C * DH_C
    kc = cache_k.reshape(n_layers * n_pool, PAGE_SIZE, H_C, DH_C)
    vc = cache_v.reshape(n_layers * n_pool, PAGE_SIZE, H_C, DH_C)

    def page(i, hb):
        return lambda b, p, pt: (layer * n_pool + pt[b, n_pages - PP * (p + 1) + i], 0, hb, 0)

    page_specs = [pl.BlockSpec((1, PAGE_SIZE, SUBLANE, DH_C), page(i, hb))
                  for i in range(PP) for hb in range(NHB)]
    grid_spec = pltpu.PrefetchScalarGridSpec(
        num_scalar_prefetch=1,
        grid=(DB, n_steps),
        in_specs=[
            pl.BlockSpec(memory_space=pltpu.SMEM),
            pl.BlockSpec((DS, D), lambda b, p, pt: (row_block0 + b, 0)),
            pl.BlockSpec((DS, D), lambda b, p, pt: (row_block0 + b, 1)),
            pl.BlockSpec((DS, D), lambda b, p, pt: (row_block0 + b, 2)),
        ] + page_specs + page_specs,
        out_specs=pl.BlockSpec((DS, D), lambda b, p, pt: (b, 0)),
        scratch_shapes=[pltpu.VMEM((D, H_C * DS), BF16),
                        pltpu.VMEM((1, H_C * DS), F32),
                        pltpu.VMEM((H_C * DS, D), F32)],
    )
    return pl.pallas_call(
        functools.partial(_sb_sample_kernel, DS=DS, n_steps=n_steps, PP=PP),
        grid_spec=grid_spec,
        out_shape=jax.ShapeDtypeStruct((DB * DS, D), BF16),
        compiler_params=_params("parallel", "arbitrary"),
        name="sb_sample_attn",
    )(page_table, bias, QKV, QKV, QKV, *([kc] * (PP * NHB)), *([vc] * (PP * NHB)))


def _top_desc(x, n):
    rows = lax.broadcasted_iota(jnp.int32, x.shape, 0)
    big = jnp.int32(x.shape[0])
    vals = []
    for _ in range(n):
        m = jnp.max(x, axis=0, keepdims=True)
        first = jnp.min(jnp.where(x == m, rows, big), axis=0, keepdims=True)
        x = jnp.where(rows == first, -jnp.inf, x)
        vals.append(m)
    return vals


def _top_desc_distinct(x, n, n_pad):
    vals = []
    for _ in range(n):
        m = jnp.max(x, axis=0, keepdims=True)
        x = jnp.where(x == m, -jnp.inf, x)
        vals.append(m)
    dropped = jnp.sum(jnp.where(x == -jnp.inf, 1.0, 0.0), axis=0, keepdims=True)
    return vals, jnp.where(dropped != float(n + n_pad), 1.0, 0.0)


def _peer_route_kernel(q_ref, keys_ref, s1_ref, s2_ref, st_ref, *, TR):
    K = PEER_TOPK
    half = PEER_DQ // 2

    def run(exact):
        def top(x, n_pad):
            if exact:
                return _top_desc(x, K), None
            return _top_desc_distinct(x, K, n_pad)

        flags = jnp.zeros((1, TR), F32)
        for h in range(PEER_HEADS):
            tops = []
            for c, s_ref in enumerate((s1_ref, s2_ref)):
                hc = 2 * h + c
                qh = q_ref[:, hc * half:(hc + 1) * half].astype(BF16)
                s = lax.dot_general(keys_ref[hc], qh, NT_DIMS, preferred_element_type=F32)
                s_ref[h] = s
                vals, bad = top(s, 0)
                tops.append(vals)
                if bad is not None:
                    flags = jnp.maximum(flags, bad)
            v1, v2 = tops
            cand = []
            for a in range(K):
                cand.extend(v1[a] + v2[b] for b in range(K // (a + 1)))
            pad = (-len(cand)) % SUBLANE
            cand = jnp.concatenate(cand + [jnp.full((pad, TR), -jnp.inf, F32)], axis=0)
            vals, bad = top(cand, pad)
            if bad is not None:
                flags = jnp.maximum(flags, bad)
            thr = vals[-1]
            m = v1[0] + v2[0]
            zsum = jnp.sum(jnp.where(cand >= thr, jnp.exp(cand - m), 0.0), axis=0, keepdims=True)
            st_ref[h] = jnp.concatenate(
                [thr, v1[0], v2[0], 1.0 / zsum, jnp.zeros((4, TR), F32)], axis=0)
        return flags

    flags = run(exact=False)

    @pl.when(jnp.max(flags) > 0.0)
    def _():
        run(exact=True)


def peer_route(q, keys_bf16, TR):
    T = q.shape[0]
    s_spec = pl.BlockSpec((PEER_HEADS, PEER_NKEYS, TR), lambda i: (0, 0, i))
    s_shape = jax.ShapeDtypeStruct((PEER_HEADS, PEER_NKEYS, T), F32)
    return pl.pallas_call(
        functools.partial(_peer_route_kernel, TR=TR),
        grid=(T // TR,),
        in_specs=[pl.BlockSpec((TR, PEER_HEADS * PEER_DQ), lambda i: (i, 0)),
                  pl.BlockSpec((2 * PEER_HEADS, PEER_NKEYS, PEER_DQ // 2), lambda i: (0, 0, 0))],
        out_specs=[s_spec, s_spec, pl.BlockSpec((PEER_HEADS, 8, TR), lambda i: (0, 0, i))],
        out_shape=[s_shape, s_shape, jax.ShapeDtypeStruct((PEER_HEADS, 8, T), F32)],
        compiler_params=_params("parallel"),
        name="peer_route",
    )(q, keys_bf16)


GATE_ROWS = 32
PEER_I1_BLOCK = SUBLANE
GELU_C0 = 0.7978845608028654
GELU_C1 = 0.7978845608028654 * 0.044715


def _peer_dense_kernel(h_ref, u_ref, vt_ref, s1_ref, s2_ref, st_ref, o_ref, e2_ref,
                       a0_ref, a1_ref, p0_ref, p1_ref, *, R, TB):
    j = pl.program_id(1)

    @pl.when(j == 0)
    def _():
        o_ref[...] = jnp.zeros_like(o_ref)
        for h in range(PEER_HEADS):
            e2_ref[h] = jnp.exp(s2_ref[h] - st_ref[h, 2:3, :]) * (0.5 * st_ref[h, 3:4, :])

    RH = R // 2
    acc = None
    for hf, (a_ref, p_ref) in enumerate(((a0_ref, p0_ref), (a1_ref, p1_ref))):
        e0 = hf * RH * PEER_NKEYS
        a_ref[...] = lax.dot_general(u_ref[e0:e0 + RH * PEER_NKEYS, :], h_ref[...], NT_DIMS,
                                     preferred_element_type=F32)
        for r in range(RH):
            rr = hf * RH + r
            for lc in range(TB // LANE):
                lanes = slice(lc * LANE, (lc + 1) * LANE)
                rowv = []
                for h in range(PEER_HEADS):
                    s1 = s1_ref[h, rr:rr + 1, lanes]
                    rowv.append((s1, jnp.exp(s1 - st_ref[h, 1:2, lanes]), st_ref[h, 0:1, lanes]))
                for g in range(PEER_NKEYS // GATE_ROWS):
                    sub = slice(g * GATE_ROWS, (g + 1) * GATE_ROWS)
                    gate = jnp.zeros((GATE_ROWS, LANE), F32)
                    for h in range(PEER_HEADS):
                        s1, e1, thr = rowv[h]
                        sel = (s1 + s2_ref[h, sub, lanes]) >= thr
                        gate = gate + jnp.where(sel, e1 * e2_ref[h, sub, lanes], 0.0)
                    rows = slice(r * PEER_NKEYS + g * GATE_ROWS, r * PEER_NKEYS + (g + 1) * GATE_ROWS)
                    a = a_ref[rows, lanes]
                    t = jnp.tanh(a * (GELU_C0 + GELU_C1 * (a * a)))
                    p_ref[rows, lanes] = (gate * (a + a * t)).astype(BF16)
        part = jnp.dot(vt_ref[:, e0:e0 + RH * PEER_NKEYS], p_ref[...], preferred_element_type=F32)
        acc = part if acc is None else acc + part
    o_ref[...] += acc


def peer_dense(h, u_bf16, vt_bf16, s1, s2, st, TB):
    T, D = h.shape
    R = PEER_I1_BLOCK
    half_shape = (R // 2 * PEER_NKEYS, TB)
    return pl.pallas_call(
        functools.partial(_peer_dense_kernel, R=R, TB=TB),
        grid=(T // TB, PEER_NKEYS // R),
        in_specs=[pl.BlockSpec((TB, D), lambda i, j: (i, 0)),
                  pl.BlockSpec((R * PEER_NKEYS, D), lambda i, j: (j, 0)),
                  pl.BlockSpec((D, R * PEER_NKEYS), lambda i, j: (0, j)),
                  pl.BlockSpec((PEER_HEADS, R, TB), lambda i, j: (0, j, i)),
                  pl.BlockSpec((PEER_HEADS, PEER_NKEYS, TB), lambda i, j: (0, 0, i)),
                  pl.BlockSpec((PEER_HEADS, 8, TB), lambda i, j: (0, 0, i))],
        out_specs=pl.BlockSpec((D, TB), lambda i, j: (0, i)),
        out_shape=jax.ShapeDtypeStruct((D, T), F32),
        scratch_shapes=[pltpu.VMEM((PEER_HEADS, PEER_NKEYS, TB), F32),
                        pltpu.VMEM(half_shape, F32), pltpu.VMEM(half_shape, F32),
                        pltpu.VMEM(half_shape, BF16), pltpu.VMEM(half_shape, BF16)],
        compiler_params=_params("parallel", "arbitrary"),
        name="peer_dense",
    )(h, u_bf16, vt_bf16, s1, s2, st)


def peer_layer(x, g, wq, keys, u, v, TB):
    q, hn = rms_matmul(x, g, wq.astype(BF16), TB, 512, emit_h=True)
    keys_b = keys.reshape(2 * PEER_HEADS, PEER_NKEYS, PEER_DQ // 2).astype(BF16)
    s1, s2, st = peer_route(q, keys_b, 256)
    yt = peer_dense(hn, u.astype(BF16), v.T.astype(BF16), s1, s2, st, TB)
    return transpose_add(x, yt, TB)


def _pick_tile(n, cap, mult):
    best = mult
    for t in range(mult, cap + 1, mult):
        if n % t == 0:
            best = t
    return best


def kernel(x_prompt, x_sample, state_gla, state_conv, cache_k, cache_v, page_table, meta_tokens, norm_mix, norm_ffn, norm_final, w_in_even, w_gate_lr, b_gate_lr, gla_norm, conv_w, conv_b, conv_norm_g, conv_norm_b, w_out_even, w_qkv_odd, w_out_odd, sb_bias, peer_wq, peer_keys, peer_u, peer_v):
    B, SEQ, D = x_prompt.shape
    DB, DS, _ = x_sample.shape
    depth = norm_mix.shape[0]
    L = N_META + SEQ
    Tp, Ts = B * L, DB * DS
    T = Tp + Ts
    TB = 768 if T >= 768 else 256
    Tpad = -(-T // TB) * TB
    GC = 48 if L % 48 == 0 else 16
    assert L % GC == 0 and Tp % DS == 0 and DS % SUBLANE == 0 and DS <= 16
    TQ = _pick_tile(SEQ, 256, LANE)
    assert SEQ % TQ == 0

    xp = jnp.concatenate([jnp.broadcast_to(meta_tokens[None], (B, N_META, D)), x_prompt], axis=1)
    X = jnp.concatenate([xp.reshape(Tp, D), x_sample.reshape(Ts, D), jnp.zeros((Tpad - T, D), F32)], axis=0)

    gla_p, gla_s, conv_p, conv_s, k_p, v_p, k_s, v_s = [], [], [], [], [], [], [], []
    for layer in range(depth):
        i = layer // 2
        if layer % 2 == 0:
            w = w_in_even[i]
            n_main = 2 * QA + 2 * VA
            w_main = jnp.concatenate([w[:, :n_main], w[:, n_main + LOWRANK_A:]], axis=1).astype(BF16)
            w_lr = jnp.pad(w[:, n_main:n_main + LOWRANK_A], ((0, 0), (0, LANE - LOWRANK_A))).astype(BF16)
            P = rms_matmul(X, norm_mix[layer], w_main, TB, 512)
            LR = rms_matmul(X, norm_mix[layer], w_lr, TB, LANE)
            wlr = jnp.pad(w_gate_lr[i], ((0, LANE - LOWRANK_A), (0, 0))).astype(BF16)
            blr = b_gate_lr[i].reshape(1, QA)
            gn = gla_norm[i].reshape(1, DV_A)
            oa_p, sp_ = gla_mixer(P, LR, wlr, blr, gn, jnp.zeros((B, H_A, DK_A, DV_A), F32),
                                  B, L // GC, GC, 0, Tp)
            oa_s, ss_ = gla_mixer(P, LR, wlr, blr, gn, state_gla[i], DB, 1, DS, Tp // DS, Ts)
            cw = jnp.pad(conv_w[i], ((0, HALO - CONV_W), (0, 0)))
            cvec = [t.reshape(1, C_B) for t in (conv_b[i], conv_norm_g[i], conv_norm_b[i])]
            ob_p, tail_p = conv_mixer(P, jnp.zeros((B, HALO, C_B), F32), cw, *cvec, B, L // GC, GC, 0, Tp)
            halo_s = jnp.pad(state_conv[i], ((0, 0), (HALO - (CONV_W - 1), 0), (0, 0)))
            ob_s, tail_s = conv_mixer(P, halo_s, cw, *cvec, DB, 1, DS, Tp // DS, Ts)
            mix = jnp.concatenate([jnp.concatenate([oa_p, ob_p], axis=1),
                                   jnp.concatenate([oa_s, ob_s], axis=1),
                                   jnp.zeros((Tpad - T, VA + C_B), BF16)], axis=0)
            X = matmul_res(mix, w_out_even[i].astype(BF16), X, TB, 512)
            gla_p.append(sp_)
            gla_s.append(ss_)
            conv_p.append(tail_p[:, HALO - (CONV_W - 1):])
            conv_s.append(tail_s[:, HALO - (CONV_W - 1):])
        else:
            QKV = rms_matmul(X, norm_mix[layer], w_qkv_odd[i].astype(BF16), TB, 512)
            HD = H_C * DH_C
            o_p = sb_prompt_attn(QKV, sb_bias[i], B, L, TQ)
            o_s = sb_sample_attn(QKV, sb_bias[i], cache_k, cache_v, i, page_table, DB, DS, Tp // DS)
            att = jnp.concatenate([o_p, o_s, jnp.zeros((Tpad - T, HD), BF16)], axis=0)
            X = matmul_res(att, w_out_odd[i].astype(BF16), X, TB, 512)
            k_p.append(QKV[:Tp, HD:2 * HD].reshape(B, L, H_C, DH_C))
            v_p.append(QKV[:Tp, 2 * HD:].reshape(B, L, H_C, DH_C))
            k_s.append(QKV[Tp:T, HD:2 * HD].reshape(DB, DS, H_C, DH_C))
            v_s.append(QKV[Tp:T, 2 * HD:].reshape(DB, DS, H_C, DH_C))
        X = peer_layer(X, norm_ffn[layer], peer_wq[layer], peer_keys[layer], peer_u[layer], peer_v[layer], TB)

    Y = rmsnorm_rows(X, norm_final, TB)
    y_prompt = Y[:Tp].reshape(B, L, D)[:, N_META:]
    y_sample = Y[Tp:T].reshape(DB, DS, D)
    return (y_prompt, y_sample, jnp.stack(gla_p), jnp.stack(gla_s), jnp.stack(conv_p), jnp.stack(conv_s),
            jnp.stack(k_p), jnp.stack(v_p), jnp.stack(k_s), jnp.stack(v_s))
```

```python
import functools

import jax
import jax.numpy as jnp
from jax import lax
from jax.experimental import pallas as pl
from jax.experimental.pallas import tpu as pltpu

F32 = jnp.float32
BF16 = jnp.bfloat16
HIGHEST = lax.Precision.HIGHEST

N_META = 16
EPS = 1e-6
H_A = 4
DK_A = 128
DV_A = 256
LOWRANK_A = 16
GATE_TAU = 16.0
C_B = 1024
CONV_W = 31
H_C = 16
DH_C = 128
PAGE_SIZE = 128
PEER_HEADS = 8
PEER_NKEYS = 128
PEER_DQ = 256
PEER_TOPK = 16
QA = H_A * DK_A
VA = H_A * DV_A

LANE = 128
SUBLANE = 8
HALO = 32
VMEM_LIMIT = 56 * 1024 * 1024

NT_DIMS = (((1,), (1,)), ((), ()))
TN_DIMS = (((0,), (0,)), ((), ()))


def _params(*sem):
    return pltpu.CompilerParams(dimension_semantics=sem, vmem_limit_bytes=VMEM_LIMIT)


def _softplus(x):
    return jnp.maximum(x, 0.0) + jnp.log1p(jnp.exp(-jnp.abs(x)))


def _log_sigmoid(x):
    return -_softplus(-x)


def _sigmoid(x):
    return 1.0 / (1.0 + jnp.exp(-x))


def _silu(x):
    return x * _sigmoid(x)


def _gelu_tanh(x):
    c = 0.7978845608028654
    return 0.5 * x * (1.0 + jnp.tanh(c * (x + 0.044715 * (x * x * x))))


def _rms_matmul_kernel(x_ref, g_ref, w_ref, o_ref, *rest, emit_h):
    h_ref = rest[-1]

    @pl.when(pl.program_id(1) == 0)
    def _():
        x = x_ref[...]
        ms = jnp.mean(x * x, axis=-1, keepdims=True)
        h = (x * lax.rsqrt(ms + EPS) * g_ref[...]).astype(BF16)
        h_ref[...] = h
        if emit_h:
            rest[0][...] = h

    o_ref[...] = jnp.dot(h_ref[...], w_ref[...], preferred_element_type=F32)


def rms_matmul(x, g, w, tm, tn, emit_h=False):
    T, D = x.shape
    N = w.shape[1]
    out_shape = [jax.ShapeDtypeStruct((T, N), F32)]
    out_specs = [pl.BlockSpec((tm, tn), lambda i, j: (i, j))]
    if emit_h:
        out_shape.append(jax.ShapeDtypeStruct((T, D), BF16))
        out_specs.append(pl.BlockSpec((tm, D), lambda i, j: (i, 0)))
    res = pl.pallas_call(
        functools.partial(_rms_matmul_kernel, emit_h=emit_h),
        grid=(T // tm, N // tn),
        in_specs=[pl.BlockSpec((tm, D), lambda i, j: (i, 0)),
                  pl.BlockSpec((1, D), lambda i, j: (0, 0)),
                  pl.BlockSpec((D, tn), lambda i, j: (0, j))],
        out_specs=out_specs,
        out_shape=out_shape,
        scratch_shapes=[pltpu.VMEM((tm, D), BF16)],
        compiler_params=_params("parallel", "arbitrary"),
        name="rms_matmul",
    )(x, g.reshape(1, D), w)
    return res if emit_h else res[0]


def _matmul_res_kernel(a_ref, w_ref, r_ref, o_ref):
    o_ref[...] = r_ref[...] + jnp.dot(a_ref[...], w_ref[...], preferred_element_type=F32)


def matmul_res(a, w, res, tm, tn):
    T, K = a.shape
    N = w.shape[1]
    return pl.pallas_call(
        _matmul_res_kernel,
        grid=(T // tm, N // tn),
        in_specs=[pl.BlockSpec((tm, K), lambda i, j: (i, 0)),
                  pl.BlockSpec((K, tn), lambda i, j: (0, j)),
                  pl.BlockSpec((tm, tn), lambda i, j: (i, j))],
        out_specs=pl.BlockSpec((tm, tn), lambda i, j: (i, j)),
        out_shape=jax.ShapeDtypeStruct((T, N), F32),
        compiler_params=_params("parallel", "arbitrary"),
        name="matmul_res",
    )(a, w, res)


def _rmsnorm_kernel(x_ref, g_ref, o_ref):
    x = x_ref[...]
    ms = jnp.mean(x * x, axis=-1, keepdims=True)
    o_ref[...] = x * lax.rsqrt(ms + EPS) * g_ref[...]


def rmsnorm_rows(x, g, tm):
    T, D = x.shape
    return pl.pallas_call(
        _rmsnorm_kernel,
        grid=(T // tm,),
        in_specs=[pl.BlockSpec((tm, D), lambda i: (i, 0)),
                  pl.BlockSpec((1, D), lambda i: (0, 0))],
        out_specs=pl.BlockSpec((tm, D), lambda i: (i, 0)),
        out_shape=jax.ShapeDtypeStruct((T, D), F32),
        compiler_params=_params("parallel"),
        name="rmsnorm_final",
    )(x, g.reshape(1, D))


def _transpose_add_kernel(x_ref, yt_ref, o_ref):
    o_ref[...] = x_ref[...] + yt_ref[...].T


def transpose_add(x, yt, tb):
    T, D = x.shape
    return pl.pallas_call(
        _transpose_add_kernel,
        grid=(T // tb,),
        in_specs=[pl.BlockSpec((tb, D), lambda i: (i, 0)),
                  pl.BlockSpec((D, tb), lambda i: (0, i))],
        out_specs=pl.BlockSpec((tb, D), lambda i: (i, 0)),
        out_shape=jax.ShapeDtypeStruct((T, D), F32),
        compiler_params=_params("parallel"),
        name="transpose_add",
    )(x, yt)


def _gla_kernel(q_ref, k_ref, v_ref, g_ref, lr_ref, wlr_ref, blr_ref, gn_ref, s0_ref,
                o_ref, sout_ref, S_ref, *, C, sb):
    c = pl.program_id(1)

    @pl.when(c == 0)
    def _():
        S_ref[...] = s0_ref[0]

    x = jnp.dot(lr_ref[...].astype(BF16), wlr_ref[...], preferred_element_type=F32) + blr_ref[...]
    loga_all = _log_sigmoid(x) * (1.0 / GATE_TAU)
    row = lax.broadcasted_iota(jnp.int32, (C, C), 0)
    col = lax.broadcasted_iota(jnp.int32, (C, C), 1)
    tril = (row >= col).astype(F32)
    b_all = jnp.dot(tril, loga_all, precision=HIGHEST, preferred_element_type=F32)
    for h in range(H_A):
        kk = slice(h * DK_A, (h + 1) * DK_A)
        vv = slice(h * DV_A, (h + 1) * DV_A)
        o, S_new = _gla_head(q_ref[:, kk] * (DK_A ** -0.5), k_ref[:, kk], v_ref[:, vv], b_all[:, kk],
                             loga_all[:, kk], S_ref[h], C, sb)
        ms = jnp.mean(o * o, axis=-1, keepdims=True)
        on = o * lax.rsqrt(ms + EPS) * gn_ref[...]
        o_ref[:, vv] = (on * _silu(g_ref[:, vv])).astype(o_ref.dtype)
        S_ref[h] = S_new
        sout_ref[0, h] = S_new


def _gla_head(q, k, v, b, loga, S, C, sb):
    o = jnp.dot(q * jnp.exp(b), S, precision=HIGHEST, preferred_element_type=F32)

    tpos = lax.broadcasted_iota(jnp.int32, (sb, 1), 0)
    outs = []
    for i in range(C // sb):
        r0 = i * sb
        bi, qi, ki, vi = b[r0:r0 + sb], q[r0:r0 + sb], k[r0:r0 + sb], v[r0:r0 + sb]
        oi = o[r0:r0 + sb]
        if i > 0:
            piv = b[r0 - 1:r0]
            qt = qi * jnp.exp(bi - piv)
            kt = k[:r0] * jnp.exp(piv - b[:r0])
            att = lax.dot_general(qt, kt, NT_DIMS, precision=HIGHEST, preferred_element_type=F32)
            oi = oi + jnp.dot(att, v[:r0], precision=HIGHEST, preferred_element_type=F32)
        for s in range(sb):
            dec = jnp.exp(jnp.minimum(bi - bi[s:s + 1], 0.0))
            a_col = jnp.sum(qi * ki[s:s + 1] * dec, axis=-1, keepdims=True)
            a_col = jnp.where(tpos >= s, a_col, 0.0)
            oi = oi + a_col * vi[s:s + 1]
        outs.append(oi)
    o = outs[0] if len(outs) == 1 else jnp.concatenate(outs, axis=0)

    blast = b[C - 1:C]
    kd = k * jnp.exp(blast - b)
    tot = lax.dot_general(loga, jnp.ones((C, DV_A), F32), TN_DIMS, precision=HIGHEST,
                          preferred_element_type=F32)
    S_new = jnp.exp(tot) * S + lax.dot_general(kd, v, TN_DIMS, precision=HIGHEST,
                                               preferred_element_type=F32)
    return o, S_new


def gla_mixer(P, LR, wlr, blr, gn, s0, nb, nchunks, C, row_block0, rows_out):
    sb = min(16, C)
    rb = lambda b, c: row_block0 + b * nchunks + c
    return pl.pallas_call(
        functools.partial(_gla_kernel, C=C, sb=sb),
        grid=(nb, nchunks),
        in_specs=[
            pl.BlockSpec((C, QA), lambda b, c: (rb(b, c), 0)),
            pl.BlockSpec((C, QA), lambda b, c: (rb(b, c), 1)),
            pl.BlockSpec((C, VA), lambda b, c: (rb(b, c), 2 * QA // VA)),
            pl.BlockSpec((C, VA), lambda b, c: (rb(b, c), 2 * QA // VA + 1)),
            pl.BlockSpec((C, LANE), lambda b, c: (rb(b, c), 0)),
            pl.BlockSpec((LANE, QA), lambda b, c: (0, 0)),
            pl.BlockSpec((1, QA), lambda b, c: (0, 0)),
            pl.BlockSpec((1, DV_A), lambda b, c: (0, 0)),
            pl.BlockSpec((1, H_A, DK_A, DV_A), lambda b, c: (b, 0, 0, 0)),
        ],
        out_specs=[
            pl.BlockSpec((C, VA), lambda b, c: (b * nchunks + c, 0)),
            pl.BlockSpec((1, H_A, DK_A, DV_A), lambda b, c: (b, 0, 0, 0)),
        ],
        out_shape=[jax.ShapeDtypeStruct((rows_out, VA), BF16),
                   jax.ShapeDtypeStruct((nb, H_A, DK_A, DV_A), F32)],
        scratch_shapes=[pltpu.VMEM((H_A, DK_A, DV_A), F32)],
        compiler_params=_params("parallel", "arbitrary"),
        name="gla_mixer",
    )(P, P, P, P, LR, wlr, blr, gn, s0)


def _conv_kernel(ga_ref, gb_ref, halo_ref, cw_ref, cb_ref, lg_ref, lb_ref, o_ref, tail_ref, u_ref, *, R):
    @pl.when(pl.program_id(1) == 0)
    def _():
        u_ref[0:HALO] = halo_ref[0]

    u_ref[HALO:HALO + R] = ga_ref[...] * _sigmoid(gb_ref[...])
    acc = jnp.zeros((R, C_B), F32) + cb_ref[...]
    first = HALO - (CONV_W - 1)
    for w in range(CONV_W):
        acc = acc + u_ref[first + w:first + w + R, :] * cw_ref[w:w + 1, :]
    mu = jnp.mean(acc, axis=-1, keepdims=True)
    d = acc - mu
    var = jnp.mean(d * d, axis=-1, keepdims=True)
    y = d * lax.rsqrt(var + EPS) * lg_ref[...] + lb_ref[...]
    o_ref[...] = _silu(y).astype(o_ref.dtype)
    new_halo = u_ref[R:R + HALO]
    u_ref[0:HALO] = new_halo
    tail_ref[0] = new_halo


def conv_mixer(P, halo, cw, cb, lg, lb, nb, nchunks, R, row_block0, rows_out):
    rb = lambda b, c: row_block0 + b * nchunks + c
    ga_blk = (2 * QA + 2 * VA) // C_B
    return pl.pallas_call(
        functools.partial(_conv_kernel, R=R),
        grid=(nb, nchunks),
        in_specs=[
            pl.BlockSpec((R, C_B), lambda b, c: (rb(b, c), ga_blk)),
            pl.BlockSpec((R, C_B), lambda b, c: (rb(b, c), ga_blk + 1)),
            pl.BlockSpec((1, HALO, C_B), lambda b, c: (b, 0, 0)),
            pl.BlockSpec((HALO, C_B), lambda b, c: (0, 0)),
            pl.BlockSpec((1, C_B), lambda b, c: (0, 0)),
            pl.BlockSpec((1, C_B), lambda b, c: (0, 0)),
            pl.BlockSpec((1, C_B), lambda b, c: (0, 0)),
        ],
        out_specs=[
            pl.BlockSpec((R, C_B), lambda b, c: (b * nchunks + c, 0)),
            pl.BlockSpec((1, HALO, C_B), lambda b, c: (b, 0, 0)),
        ],
        out_shape=[jax.ShapeDtypeStruct((rows_out, C_B), BF16),
                   jax.ShapeDtypeStruct((nb, HALO, C_B), F32)],
        scratch_shapes=[pltpu.VMEM((HALO + R, C_B), F32)],
        compiler_params=_params("parallel", "arbitrary"),
        name="conv_mixer",
    )(P, P, halo, cw, cb, lg, lb)


def _sb_tile(z, R, vt, tri, mask):
    sp = _softplus(z)
    if mask is not None:
        sp = jnp.where(mask, sp, 0.0)
    hi = sp.astype(BF16)
    lo = (sp - hi.astype(F32)).astype(BF16)
    aft = (jnp.dot(hi, tri, preferred_element_type=F32)
           + jnp.dot(lo, tri, preferred_element_type=F32) + R)
    a = jnp.exp(z - sp - aft)
    if mask is not None:
        a = jnp.where(mask, a, 0.0)
    contrib = jnp.dot(a.astype(BF16), vt, preferred_element_type=F32)
    return contrib, aft[:, 0:1] + sp[:, 0:1]


SB_HEADS_PER_STEP = 2


def _sb_prompt_kernel(bias_ref, q_ref, k_ref, v_ref, o_ref, *, L, TQ):
    M = N_META
    HPS = SB_HEADS_PER_STEP
    heads = range(HPS)
    hcol = [slice(i * DH_C, (i + 1) * DH_C) for i in heads]
    bias = [bias_ref[pl.program_id(1) * HPS + i] for i in heads]

    def tri_and_causal(n):
        row = lax.broadcasted_iota(jnp.int32, (n, n), 0)
        col = lax.broadcasted_iota(jnp.int32, (n, n), 1)
        return (row > col).astype(BF16), col < row

    def qrows(r0, n):
        return [(q_ref[r0:r0 + n, hcol[i]] * (DH_C ** -0.5)).astype(BF16) for i in heads]

    def tiles(qts, k0, n, Rs, tri, mask):
        out = []
        for i in heads:
            kt = k_ref[pl.ds(k0, n), hcol[i]].astype(BF16)
            vt = v_ref[pl.ds(k0, n), hcol[i]].astype(BF16)
            z = lax.dot_general(qts[i], kt, NT_DIMS, preferred_element_type=F32) + bias[i]
            out.append(_sb_tile(z, Rs[i], vt, tri, mask))
        return [o[0] for o in out], [o[1] for o in out]

    tri_m, causal_m = tri_and_causal(M)
    tri_q, causal_q = tri_and_causal(TQ)

    accs, _ = tiles(qrows(0, M), 0, M, [jnp.zeros((M, 1), F32)] * HPS, tri_m, causal_m)
    for i in heads:
        o_ref[0:M, hcol[i]] = accs[i].astype(o_ref.dtype)

    for qi in range((L - M) // TQ):
        q0 = M + qi * TQ
        qts = qrows(q0, TQ)
        accs, Rs = tiles(qts, q0, TQ, [jnp.zeros((TQ, 1), F32)] * HPS, tri_q, causal_q)

        def body(j, carry, qts=qts, qi=qi):
            accs, Rs = carry
            k0 = pl.multiple_of(M + (qi - 1 - j) * TQ, SUBLANE)
            contribs, Rs = tiles(qts, k0, TQ, Rs, tri_q, None)
            return [a + c for a, c in zip(accs, contribs)], Rs

        if qi > 0:
            accs, Rs = lax.fori_loop(0, qi, body, (accs, Rs))
        contribs, _ = tiles(qts, 0, M, Rs, tri_m, None)
        for i in heads:
            o_ref[q0:q0 + TQ, hcol[i]] = (accs[i] + contribs[i]).astype(o_ref.dtype)


def sb_prompt_attn(QKV, bias, nb, L, TQ):
    HPS = SB_HEADS_PER_STEP
    W = HPS * DH_C
    nh = H_C // HPS
    return pl.pallas_call(
        functools.partial(_sb_prompt_kernel, L=L, TQ=TQ),
        grid=(nb, nh),
        in_specs=[
            pl.BlockSpec(memory_space=pltpu.SMEM),
            pl.BlockSpec((L, W), lambda b, h: (b, h)),
            pl.BlockSpec((L, W), lambda b, h: (b, nh + h)),
            pl.BlockSpec((L, W), lambda b, h: (b, 2 * nh + h)),
        ],
        out_specs=pl.BlockSpec((L, W), lambda b, h: (b, h)),
        out_shape=jax.ShapeDtypeStruct((nb * L, H_C * DH_C), BF16),
        compiler_params=_params("parallel", "parallel"),
        name="sb_prompt_attn",
    )(bias, QKV, QKV, QKV)


def _sb_sample_kernel(pt_ref, bias_ref, q_ref, kn_ref, vn_ref, kc_ref, vc_ref, o_ref,
                      qbd_ref, R_ref, acc_ref, kbuf, vbuf, sem, *, DS, n_steps, n_pages, PP, page0):
    b = pl.program_id(0)
    p = pl.program_id(1)
    HQ = H_C * DS
    D = H_C * DH_C
    NK = PP * PAGE_SIZE
    step = b * n_steps + p
    n_total = pl.num_programs(0) * n_steps

    def page_copies(g, slot):
        gb = g // n_steps
        gp = g - gb * n_steps
        cps = []
        for i in range(PP):
            page = page0 + pt_ref[gb, n_pages - PP * (gp + 1) + i]
            for c, (src, dst) in enumerate(((kc_ref, kbuf), (vc_ref, vbuf))):
                for h in range(H_C):
                    cps.append(pltpu.make_async_copy(src.at[page, :, h, :], dst.at[slot, i, h],
                                                     sem.at[c, slot]))
        return cps

    slot = step % 2

    @pl.when(step == 0)
    def _():
        for cp in page_copies(step, slot):
            cp.start()

    @pl.when(step + 1 < n_total)
    def _():
        for cp in page_copies(step + 1, 1 - slot):
            cp.start()

    def bias_row():
        lane = lax.broadcasted_iota(jnp.int32, (1, HQ), 1)
        out = jnp.zeros((1, HQ), F32)
        for h in range(H_C):
            out = jnp.where(lane // DS == h, bias_ref[h], out)
        return out

    def page_rows(buf, i):
        return jnp.concatenate([buf[slot, i, h] for h in range(H_C)], axis=1).astype(BF16)

    @pl.when(p == 0)
    def _():
        qs = q_ref[...] * (DH_C ** -0.5)
        eye = (lax.broadcasted_iota(jnp.int32, (DS, DS), 0)
               == lax.broadcasted_iota(jnp.int32, (DS, DS), 1)).astype(F32)
        qT = lax.dot_general(qs, eye, TN_DIMS, precision=HIGHEST, preferred_element_type=F32)
        rep = jnp.concatenate([qT] * H_C, axis=1)
        r_head = lax.broadcasted_iota(jnp.int32, (D, HQ), 0) // DH_C
        c_head = lax.broadcasted_iota(jnp.int32, (D, HQ), 1) // DS
        qbd = jnp.where(r_head == c_head, rep, 0.0).astype(BF16)
        qbd_ref[...] = qbd
        z = jnp.dot(kn_ref[...].astype(BF16), qbd, preferred_element_type=F32) + bias_row()
        jrow = lax.broadcasted_iota(jnp.int32, (DS, HQ), 0)
        tq = lax.broadcasted_iota(jnp.int32, (DS, HQ), 1) % DS
        mask = jrow < tq
        sp = jnp.where(mask, _softplus(z), 0.0)
        triu = (lax.broadcasted_iota(jnp.int32, (DS, DS), 1)
                > lax.broadcasted_iota(jnp.int32, (DS, DS), 0)).astype(F32)
        aft = jnp.dot(triu, sp, precision=HIGHEST, preferred_element_type=F32)
        a = jnp.where(mask, jnp.exp(z - sp - aft), 0.0)
        acc_ref[...] = lax.dot_general(a.astype(BF16), vn_ref[...].astype(BF16), TN_DIMS,
                                       preferred_element_type=F32)
        R_ref[...] = jnp.sum(sp, axis=0, keepdims=True)

    for cp in page_copies(step, slot):
        cp.wait()
    kp = jnp.concatenate([page_rows(kbuf, i) for i in range(PP)], axis=0)
    vp = jnp.concatenate([page_rows(vbuf, i) for i in range(PP)], axis=0)
    z = jnp.dot(kp, qbd_ref[...], preferred_element_type=F32) + bias_row()
    sp = _softplus(z)
    hi = sp.astype(BF16)
    lo = (sp - hi.astype(F32)).astype(BF16)
    triu = (lax.broadcasted_iota(jnp.int32, (NK, NK), 1)
            > lax.broadcasted_iota(jnp.int32, (NK, NK), 0)).astype(BF16)
    aft = (jnp.dot(triu, hi, preferred_element_type=F32)
           + jnp.dot(triu, lo, preferred_element_type=F32) + R_ref[...])
    a = jnp.exp(z - sp - aft)
    acc_ref[...] += lax.dot_general(a.astype(BF16), vp, TN_DIMS, preferred_element_type=F32)
    R_ref[...] = aft[0:1, :] + sp[0:1, :]

    @pl.when(p == n_steps - 1)
    def _():
        for h in range(H_C):
            o_ref[:, h * DH_C:(h + 1) * DH_C] = acc_ref[h * DS:(h + 1) * DS,
                                                        h * DH_C:(h + 1) * DH_C].astype(o_ref.dtype)


def sb_sample_attn(QKV, bias, cache_k, cache_v, layer, page_table, DB, DS, row_block0):
    n_layers, n_pool = cache_k.shape[:2]
    n_pages = page_table.shape[1]
    PP = 2 if n_pages % 2 == 0 else 1
    n_steps = n_pages // PP
    D = H_C * DH_C
    kc = cache_k.reshape(n_layers * n_pool, PAGE_SIZE, H_C, DH_C)
    vc = cache_v.reshape(n_layers * n_pool, PAGE_SIZE, H_C, DH_C)
    buf = pltpu.VMEM((2, PP, H_C, PAGE_SIZE, DH_C), F32)
    grid_spec = pltpu.PrefetchScalarGridSpec(
        num_scalar_prefetch=1,
        grid=(DB, n_steps),
        in_specs=[
            pl.BlockSpec(memory_space=pltpu.SMEM),
            pl.BlockSpec((DS, D), lambda b, p, pt: (row_block0 + b, 0)),
            pl.BlockSpec((DS, D), lambda b, p, pt: (row_block0 + b, 1)),
            pl.BlockSpec((DS, D), lambda b, p, pt: (row_block0 + b, 2)),
            pl.BlockSpec(memory_space=pl.ANY),
            pl.BlockSpec(memory_space=pl.ANY),
        ],
        out_specs=pl.BlockSpec((DS, D), lambda b, p, pt: (b, 0)),
        scratch_shapes=[pltpu.VMEM((D, H_C * DS), BF16),
                        pltpu.VMEM((1, H_C * DS), F32),
                        pltpu.VMEM((H_C * DS, D), F32),
                        buf, buf, pltpu.SemaphoreType.DMA((2, 2))],
    )
    return pl.pallas_call(
        functools.partial(_sb_sample_kernel, DS=DS, n_steps=n_steps, n_pages=n_pages, PP=PP,
                          page0=layer * n_pool),
        grid_spec=grid_spec,
        out_shape=jax.ShapeDtypeStruct((DB * DS, D), BF16),
        compiler_params=_params("arbitrary", "arbitrary"),
        name="sb_sample_attn",
    )(page_table, bias, QKV, QKV, QKV, kc, vc)


def _top_desc(x, n):
    rows = lax.broadcasted_iota(jnp.int32, x.shape, 0)
    big = jnp.int32(x.shape[0])
    vals = []
    for _ in range(n):
        m = jnp.max(x, axis=0, keepdims=True)
        first = jnp.min(jnp.where(x == m, rows, big), axis=0, keepdims=True)
        x = jnp.where(rows == first, -jnp.inf, x)
        vals.append(m)
    return vals


def _top_desc_distinct(x, n, n_pad):
    vals = []
    for _ in range(n):
        m = jnp.max(x, axis=0, keepdims=True)
        x = jnp.where(x == m, -jnp.inf, x)
        vals.append(m)
    dropped = jnp.sum(jnp.where(x == -jnp.inf, 1.0, 0.0), axis=0, keepdims=True)
    return vals, jnp.where(dropped != float(n + n_pad), 1.0, 0.0)


def _peer_route_kernel(q_ref, keys_ref, s1_ref, s2_ref, st_ref, *, TR):
    K = PEER_TOPK
    half = PEER_DQ // 2

    def run(exact):
        def top(x, n_pad):
            if exact:
                return _top_desc(x, K + 1), None
            return _top_desc_distinct(x, K + 1, n_pad)

        flags = jnp.zeros((1, TR), F32)
        for h in range(PEER_HEADS):
            tops = []
            for c, s_ref in enumerate((s1_ref, s2_ref)):
                hc = 2 * h + c
                qh = q_ref[:, hc * half:(hc + 1) * half].astype(BF16)
                s = lax.dot_general(keys_ref[hc], qh, NT_DIMS, preferred_element_type=F32)
                s_ref[h] = s
                vals, bad = top(s, 0)
                tops.append(vals)
                if bad is not None:
                    flags = jnp.maximum(flags, bad)
            v1, v2 = tops
            cand = []
            for a in range(K + 1):
                cand.extend(v1[a] + v2[b] for b in range((K + 1) // (a + 1)))
            pad = (-len(cand)) % SUBLANE
            cand = jnp.concatenate(cand + [jnp.full((pad, TR), -jnp.inf, F32)], axis=0)
            vals, bad = top(cand, pad)
            if bad is not None:
                flags = jnp.maximum(flags, bad)
            thr = 0.5 * (vals[K - 1] + vals[K])
            m = v1[0] + v2[0]
            zsum = jnp.sum(jnp.where(cand >= thr, jnp.exp(cand - m), 0.0), axis=0, keepdims=True)
            st_ref[h] = jnp.concatenate(
                [thr, v1[0], v2[0], 1.0 / zsum, jnp.zeros((4, TR), F32)], axis=0)
        return flags

    flags = run(exact=False)

    @pl.when(jnp.max(flags) > 0.0)
    def _():
        run(exact=True)


def peer_route(q, keys_bf16, TR):
    T = q.shape[0]
    s_spec = pl.BlockSpec((PEER_HEADS, PEER_NKEYS, TR), lambda i: (0, 0, i))
    s_shape = jax.ShapeDtypeStruct((PEER_HEADS, PEER_NKEYS, T), F32)
    return pl.pallas_call(
        functools.partial(_peer_route_kernel, TR=TR),
        grid=(T // TR,),
        in_specs=[pl.BlockSpec((TR, PEER_HEADS * PEER_DQ), lambda i: (i, 0)),
                  pl.BlockSpec((2 * PEER_HEADS, PEER_NKEYS, PEER_DQ // 2), lambda i: (0, 0, 0))],
        out_specs=[s_spec, s_spec, pl.BlockSpec((PEER_HEADS, 8, TR), lambda i: (0, 0, i))],
        out_shape=[s_shape, s_shape, jax.ShapeDtypeStruct((PEER_HEADS, 8, T), F32)],
        compiler_params=_params("parallel"),
        name="peer_route",
    )(q, keys_bf16)


GATE_ROWS = 32
PEER_I1_BLOCK = SUBLANE
GELU_C0 = 0.7978845608028654
GELU_C1 = 0.7978845608028654 * 0.044715


def _peer_dense_kernel(h_ref, u_ref, vt_ref, s1_ref, s2_ref, st_ref, o_ref, e2_ref,
                       a0_ref, a1_ref, p0_ref, p1_ref, *, R, TB):
    j = pl.program_id(1)

    @pl.when(j == 0)
    def _():
        o_ref[...] = jnp.zeros_like(o_ref)
        for h in range(PEER_HEADS):
            e2_ref[h] = jnp.exp(s2_ref[h] - st_ref[h, 2:3, :]) * (0.5 * st_ref[h, 3:4, :])

    RH = R // 2
    acc = None
    for hf, (a_ref, p_ref) in enumerate(((a0_ref, p0_ref), (a1_ref, p1_ref))):
        e0 = hf * RH * PEER_NKEYS
        a_ref[...] = lax.dot_general(u_ref[e0:e0 + RH * PEER_NKEYS, :], h_ref[...], NT_DIMS,
                                     preferred_element_type=F32)
        for r in range(RH):
            rr = hf * RH + r
            for lc in range(TB // LANE):
                lanes = slice(lc * LANE, (lc + 1) * LANE)
                rowv = []
                for h in range(PEER_HEADS):
                    s1 = s1_ref[h, rr:rr + 1, lanes]
                    rowv.append((st_ref[h, 0:1, lanes] - s1, jnp.exp(s1 - st_ref[h, 1:2, lanes])))
                for g in range(PEER_NKEYS // GATE_ROWS):
                    sub = slice(g * GATE_ROWS, (g + 1) * GATE_ROWS)
                    gate = jnp.zeros((GATE_ROWS, LANE), F32)
                    for h in range(PEER_HEADS):
                        need, e1 = rowv[h]
                        gate = gate + jnp.where(s2_ref[h, sub, lanes] >= need, e1 * e2_ref[h, sub, lanes], 0.0)
                    rows = slice(r * PEER_NKEYS + g * GATE_ROWS, r * PEER_NKEYS + (g + 1) * GATE_ROWS)
                    a = a_ref[rows, lanes]
                    t = jnp.tanh(a * (GELU_C0 + GELU_C1 * (a * a)))
                    p_ref[rows, lanes] = (gate * (a + a * t)).astype(BF16)
        part = jnp.dot(vt_ref[:, e0:e0 + RH * PEER_NKEYS], p_ref[...], preferred_element_type=F32)
        acc = part if acc is None else acc + part
    o_ref[...] += acc


def peer_dense(h, u_bf16, vt_bf16, s1, s2, st, TB):
    T, D = h.shape
    R = PEER_I1_BLOCK
    half_shape = (R // 2 * PEER_NKEYS, TB)
    return pl.pallas_call(
        functools.partial(_peer_dense_kernel, R=R, TB=TB),
        grid=(T // TB, PEER_NKEYS // R),
        in_specs=[pl.BlockSpec((TB, D), lambda i, j: (i, 0)),
                  pl.BlockSpec((R * PEER_NKEYS, D), lambda i, j: (j, 0)),
                  pl.BlockSpec((D, R * PEER_NKEYS), lambda i, j: (0, j)),
                  pl.BlockSpec((PEER_HEADS, R, TB), lambda i, j: (0, j, i)),
                  pl.BlockSpec((PEER_HEADS, PEER_NKEYS, TB), lambda i, j: (0, 0, i)),
                  pl.BlockSpec((PEER_HEADS, 8, TB), lambda i, j: (0, 0, i))],
        out_specs=pl.BlockSpec((D, TB), lambda i, j: (0, i)),
        out_shape=jax.ShapeDtypeStruct((D, T), F32),
        scratch_shapes=[pltpu.VMEM((PEER_HEADS, PEER_NKEYS, TB), F32),
                        pltpu.VMEM(half_shape, F32), pltpu.VMEM(half_shape, F32),
                        pltpu.VMEM(half_shape, BF16), pltpu.VMEM(half_shape, BF16)],
        compiler_params=_params("parallel", "arbitrary"),
        name="peer_dense",
    )(h, u_bf16, vt_bf16, s1, s2, st)


def peer_layer(x, g, wq, keys, u, v, TB):
    q, hn = rms_matmul(x, g, wq.astype(BF16), TB, 512, emit_h=True)
    keys_b = keys.reshape(2 * PEER_HEADS, PEER_NKEYS, PEER_DQ // 2).astype(BF16)
    s1, s2, st = peer_route(q, keys_b, 256)
    yt = peer_dense(hn, u.astype(BF16), v.T.astype(BF16), s1, s2, st, TB)
    return transpose_add(x, yt, TB)


def _pick_tile(n, cap, mult):
    best = mult
    for t in range(mult, cap + 1, mult):
        if n % t == 0:
            best = t
    return best


def kernel(x_prompt, x_sample, state_gla, state_conv, cache_k, cache_v, page_table, meta_tokens, norm_mix, norm_ffn, norm_final, w_in_even, w_gate_lr, b_gate_lr, gla_norm, conv_w, conv_b, conv_norm_g, conv_norm_b, w_out_even, w_qkv_odd, w_out_odd, sb_bias, peer_wq, peer_keys, peer_u, peer_v):
    B, SEQ, D = x_prompt.shape
    DB, DS, _ = x_sample.shape
    depth = norm_mix.shape[0]
    L = N_META + SEQ
    Tp, Ts = B * L, DB * DS
    T = Tp + Ts
    TB = 768 if T >= 768 else 256
    Tpad = -(-T // TB) * TB
    GC = 48 if L % 48 == 0 else 16
    assert L % GC == 0 and Tp % DS == 0 and DS % SUBLANE == 0 and DS <= 16
    TQ = _pick_tile(SEQ, 256, LANE)
    assert SEQ % TQ == 0

    xp = jnp.concatenate([jnp.broadcast_to(meta_tokens[None], (B, N_META, D)), x_prompt], axis=1)
    X = jnp.concatenate([xp.reshape(Tp, D), x_sample.reshape(Ts, D), jnp.zeros((Tpad - T, D), F32)], axis=0)

    gla_p, gla_s, conv_p, conv_s, k_p, v_p, k_s, v_s = [], [], [], [], [], [], [], []
    for layer in range(depth):
        i = layer // 2
        if layer % 2 == 0:
            w = w_in_even[i]
            n_main = 2 * QA + 2 * VA
            w_main = jnp.concatenate([w[:, :n_main], w[:, n_main + LOWRANK_A:]], axis=1).astype(BF16)
            w_lr = jnp.pad(w[:, n_main:n_main + LOWRANK_A], ((0, 0), (0, LANE - LOWRANK_A))).astype(BF16)
            P = rms_matmul(X, norm_mix[layer], w_main, TB, 512)
            LR = rms_matmul(X, norm_mix[layer], w_lr, TB, LANE)
            wlr = jnp.pad(w_gate_lr[i], ((0, LANE - LOWRANK_A), (0, 0))).astype(BF16)
            blr = b_gate_lr[i].reshape(1, QA)
            gn = gla_norm[i].reshape(1, DV_A)
            oa_p, sp_ = gla_mixer(P, LR, wlr, blr, gn, jnp.zeros((B, H_A, DK_A, DV_A), F32),
                                  B, L // GC, GC, 0, Tp)
            oa_s, ss_ = gla_mixer(P, LR, wlr, blr, gn, state_gla[i], DB, 1, DS, Tp // DS, Ts)
            cw = jnp.pad(conv_w[i], ((0, HALO - CONV_W), (0, 0)))
            cvec = [t.reshape(1, C_B) for t in (conv_b[i], conv_norm_g[i], conv_norm_b[i])]
            ob_p, tail_p = conv_mixer(P, jnp.zeros((B, HALO, C_B), F32), cw, *cvec, B, L // GC, GC, 0, Tp)
            halo_s = jnp.pad(state_conv[i], ((0, 0), (HALO - (CONV_W - 1), 0), (0, 0)))
            ob_s, tail_s = conv_mixer(P, halo_s, cw, *cvec, DB, 1, DS, Tp // DS, Ts)
            mix = jnp.concatenate([jnp.concatenate([oa_p, ob_p], axis=1),
                                   jnp.concatenate([oa_s, ob_s], axis=1),
                                   jnp.zeros((Tpad - T, VA + C_B), BF16)], axis=0)
            X = matmul_res(mix, w_out_even[i].astype(BF16), X, TB, 512)
            gla_p.append(sp_)
            gla_s.append(ss_)
            conv_p.append(tail_p[:, HALO - (CONV_W - 1):])
            conv_s.append(tail_s[:, HALO - (CONV_W - 1):])
        else:
            QKV = rms_matmul(X, norm_mix[layer], w_qkv_odd[i].astype(BF16), TB, 512)
            HD = H_C * DH_C
            o_p = sb_prompt_attn(QKV, sb_bias[i], B, L, TQ)
            o_s = sb_sample_attn(QKV, sb_bias[i], cache_k, cache_v, i, page_table, DB, DS, Tp // DS)
            att = jnp.concatenate([o_p, o_s, jnp.zeros((Tpad - T, HD), BF16)], axis=0)
            X = matmul_res(att, w_out_odd[i].astype(BF16), X, TB, 512)
            k_p.append(QKV[:Tp, HD:2 * HD].reshape(B, L, H_C, DH_C))
            v_p.append(QKV[:Tp, 2 * HD:].reshape(B, L, H_C, DH_C))
            k_s.append(QKV[Tp:T, HD:2 * HD].reshape(DB, DS, H_C, DH_C))
            v_s.append(QKV[Tp:T, 2 * HD:].reshape(DB, DS, H_C, DH_C))
        X = peer_layer(X, norm_ffn[layer], peer_wq[layer], peer_keys[layer], peer_u[layer], peer_v[layer], TB)

    Y = rmsnorm_rows(X, norm_final, TB)
    y_prompt = Y[:Tp].reshape(B, L, D)[:, N_META:]
    y_sample = Y[Tp:T].reshape(DB, DS, D)
    return (y_prompt, y_sample, jnp.stack(gla_p), jnp.stack(gla_s), jnp.stack(conv_p), jnp.stack(conv_s),
            jnp.stack(k_p), jnp.stack(v_p), jnp.stack(k_s), jnp.stack(v_s))
```

```python
import functools

import jax
import jax.numpy as jnp
from jax import lax
from jax.experimental import pallas as pl
from jax.experimental.pallas import tpu as pltpu

F32 = jnp.float32
BF16 = jnp.bfloat16
HIGHEST = lax.Precision.HIGHEST

N_META = 16
EPS = 1e-6
H_A = 4
DK_A = 128
DV_A = 256
LOWRANK_A = 16
GATE_TAU = 16.0
C_B = 1024
CONV_W = 31
H_C = 16
DH_C = 128
PAGE_SIZE = 128
PEER_HEADS = 8
PEER_NKEYS = 128
PEER_DQ = 256
PEER_TOPK = 16
QA = H_A * DK_A
VA = H_A * DV_A

LANE = 128
SUBLANE = 8
HALO = 32
VMEM_LIMIT = 56 * 1024 * 1024

NT_DIMS = (((1,), (1,)), ((), ()))
TN_DIMS = (((0,), (0,)), ((), ()))


def _params(*sem):
    return pltpu.CompilerParams(dimension_semantics=sem, vmem_limit_bytes=VMEM_LIMIT)


def _softplus(x):
    return jnp.maximum(x, 0.0) + jnp.log(1.0 + jnp.exp(-jnp.abs(x)))


def _log_sigmoid(x):
    return -_softplus(-x)


def _sigmoid(x):
    return 1.0 / (1.0 + jnp.exp(-x))


def _silu(x):
    return x * _sigmoid(x)


def _gelu_tanh(x):
    c = 0.7978845608028654
    return 0.5 * x * (1.0 + jnp.tanh(c * (x + 0.044715 * (x * x * x))))


def _rms_matmul_kernel(x_ref, g_ref, w_ref, o_ref, *rest, emit_h):
    h_ref = rest[-1]

    @pl.when(pl.program_id(1) == 0)
    def _():
        x = x_ref[...]
        ms = jnp.mean(x * x, axis=-1, keepdims=True)
        h = (x * lax.rsqrt(ms + EPS) * g_ref[...]).astype(BF16)
        h_ref[...] = h
        if emit_h:
            rest[0][...] = h

    o_ref[...] = jnp.dot(h_ref[...], w_ref[...], preferred_element_type=F32)


def rms_matmul(x, g, w, tm, tn, emit_h=False):
    T, D = x.shape
    N = w.shape[1]
    out_shape = [jax.ShapeDtypeStruct((T, N), F32)]
    out_specs = [pl.BlockSpec((tm, tn), lambda i, j: (i, j))]
    if emit_h:
        out_shape.append(jax.ShapeDtypeStruct((T, D), BF16))
        out_specs.append(pl.BlockSpec((tm, D), lambda i, j: (i, 0)))
    res = pl.pallas_call(
        functools.partial(_rms_matmul_kernel, emit_h=emit_h),
        grid=(T // tm, N // tn),
        in_specs=[pl.BlockSpec((tm, D), lambda i, j: (i, 0)),
                  pl.BlockSpec((1, D), lambda i, j: (0, 0)),
                  pl.BlockSpec((D, tn), lambda i, j: (0, j))],
        out_specs=out_specs,
        out_shape=out_shape,
        scratch_shapes=[pltpu.VMEM((tm, D), BF16)],
        compiler_params=_params("parallel", "arbitrary"),
        name="rms_matmul",
    )(x, g.reshape(1, D), w)
    return res if emit_h else res[0]


def _matmul_res_kernel(a_ref, w_ref, r_ref, o_ref):
    o_ref[...] = r_ref[...] + jnp.dot(a_ref[...], w_ref[...], preferred_element_type=F32)


def matmul_res(a, w, res, tm, tn):
    T, K = a.shape
    N = w.shape[1]
    return pl.pallas_call(
        _matmul_res_kernel,
        grid=(T // tm, N // tn),
        in_specs=[pl.BlockSpec((tm, K), lambda i, j: (i, 0)),
                  pl.BlockSpec((K, tn), lambda i, j: (0, j)),
                  pl.BlockSpec((tm, tn), lambda i, j: (i, j))],
        out_specs=pl.BlockSpec((tm, tn), lambda i, j: (i, j)),
        out_shape=jax.ShapeDtypeStruct((T, N), F32),
        compiler_params=_params("parallel", "arbitrary"),
        name="matmul_res",
    )(a, w, res)


def _rmsnorm_kernel(x_ref, g_ref, o_ref):
    x = x_ref[...]
    ms = jnp.mean(x * x, axis=-1, keepdims=True)
    o_ref[...] = x * lax.rsqrt(ms + EPS) * g_ref[...]


def rmsnorm_rows(x, g, tm):
    T, D = x.shape
    return pl.pallas_call(
        _rmsnorm_kernel,
        grid=(T // tm,),
        in_specs=[pl.BlockSpec((tm, D), lambda i: (i, 0)),
                  pl.BlockSpec((1, D), lambda i: (0, 0))],
        out_specs=pl.BlockSpec((tm, D), lambda i: (i, 0)),
        out_shape=jax.ShapeDtypeStruct((T, D), F32),
        compiler_params=_params("parallel"),
        name="rmsnorm_final",
    )(x, g.reshape(1, D))


def _transpose_add_kernel(x_ref, yt_ref, o_ref):
    o_ref[...] = x_ref[...] + yt_ref[...].T


def transpose_add(x, yt, tb):
    T, D = x.shape
    return pl.pallas_call(
        _transpose_add_kernel,
        grid=(T // tb,),
        in_specs=[pl.BlockSpec((tb, D), lambda i: (i, 0)),
                  pl.BlockSpec((D, tb), lambda i: (0, i))],
        out_specs=pl.BlockSpec((tb, D), lambda i: (i, 0)),
        out_shape=jax.ShapeDtypeStruct((T, D), F32),
        compiler_params=_params("parallel"),
        name="transpose_add",
    )(x, yt)


def _gla_kernel(q_ref, k_ref, v_ref, g_ref, lr_ref, wlr_ref, blr_ref, gn_ref, s0_ref,
                o_ref, sout_ref, S_ref, *, C, sb):
    c = pl.program_id(1)

    @pl.when(c == 0)
    def _():
        S_ref[...] = s0_ref[0]

    x = jnp.dot(lr_ref[...].astype(BF16), wlr_ref[...], preferred_element_type=F32) + blr_ref[...]
    loga_all = _log_sigmoid(x) * (1.0 / GATE_TAU)
    row = lax.broadcasted_iota(jnp.int32, (C, C), 0)
    col = lax.broadcasted_iota(jnp.int32, (C, C), 1)
    tril = (row >= col).astype(F32)
    b_all = jnp.dot(tril, loga_all, precision=HIGHEST, preferred_element_type=F32)
    H = range(H_A)
    kk = [slice(h * DK_A, (h + 1) * DK_A) for h in H]
    vv = [slice(h * DV_A, (h + 1) * DV_A) for h in H]
    q = [q_ref[:, kk[h]] * (DK_A ** -0.5) for h in H]
    k = [k_ref[:, kk[h]] for h in H]
    v = [v_ref[:, vv[h]] for h in H]
    b = [b_all[:, kk[h]] for h in H]
    S = [S_ref[h] for h in H]
    hp = dict(precision=HIGHEST, preferred_element_type=F32)

    o = [jnp.dot(q[h] * jnp.exp(b[h]), S[h], **hp) for h in H]
    tpos = lax.broadcasted_iota(jnp.int32, (sb, 1), 0)
    outs = [[] for _ in H]
    for i in range(C // sb):
        r0 = i * sb
        rows = slice(r0, r0 + sb)
        oi = [o[h][rows] for h in H]
        if i > 0:
            qt = [q[h][rows] * jnp.exp(b[h][rows] - b[h][r0 - 1:r0]) for h in H]
            kt = [k[h][:r0] * jnp.exp(b[h][r0 - 1:r0] - b[h][:r0]) for h in H]
            att = [lax.dot_general(qt[h], kt[h], NT_DIMS, **hp) for h in H]
            oi = [oi[h] + jnp.dot(att[h], v[h][:r0], **hp) for h in H]
        for s in range(sb):
            for h in H:
                bi = b[h][rows]
                dec = jnp.exp(jnp.minimum(bi - bi[s:s + 1], 0.0))
                a_col = jnp.sum(q[h][rows] * k[h][r0 + s:r0 + s + 1] * dec, axis=-1, keepdims=True)
                a_col = jnp.where(tpos >= s, a_col, 0.0)
                oi[h] = oi[h] + a_col * v[h][r0 + s:r0 + s + 1]
        for h in H:
            outs[h].append(oi[h])

    for h in H:
        oh = outs[h][0] if len(outs[h]) == 1 else jnp.concatenate(outs[h], axis=0)
        ms = jnp.mean(oh * oh, axis=-1, keepdims=True)
        on = oh * lax.rsqrt(ms + EPS) * gn_ref[...]
        o_ref[:, vv[h]] = (on * _silu(g_ref[:, vv[h]])).astype(o_ref.dtype)

    ones = jnp.ones((C, DV_A), F32)
    kd = [k[h] * jnp.exp(b[h][C - 1:C] - b[h]) for h in H]
    tot = [lax.dot_general(loga_all[:, kk[h]], ones, TN_DIMS, **hp) for h in H]
    upd = [lax.dot_general(kd[h], v[h], TN_DIMS, **hp) for h in H]
    for h in H:
        S_new = jnp.exp(tot[h]) * S[h] + upd[h]
        S_ref[h] = S_new
        sout_ref[0, h] = S_new


def gla_mixer(P, LR, wlr, blr, gn, s0, nb, nchunks, C, row_block0, rows_out):
    sb = min(16, C)
    rb = lambda b, c: row_block0 + b * nchunks + c
    return pl.pallas_call(
        functools.partial(_gla_kernel, C=C, sb=sb),
        grid=(nb, nchunks),
        in_specs=[
            pl.BlockSpec((C, QA), lambda b, c: (rb(b, c), 0)),
            pl.BlockSpec((C, QA), lambda b, c: (rb(b, c), 1)),
            pl.BlockSpec((C, VA), lambda b, c: (rb(b, c), 2 * QA // VA)),
            pl.BlockSpec((C, VA), lambda b, c: (rb(b, c), 2 * QA // VA + 1)),
            pl.BlockSpec((C, LANE), lambda b, c: (rb(b, c), 0)),
            pl.BlockSpec((LANE, QA), lambda b, c: (0, 0)),
            pl.BlockSpec((1, QA), lambda b, c: (0, 0)),
            pl.BlockSpec((1, DV_A), lambda b, c: (0, 0)),
            pl.BlockSpec((1, H_A, DK_A, DV_A), lambda b, c: (b, 0, 0, 0)),
        ],
        out_specs=[
            pl.BlockSpec((C, VA), lambda b, c: (b * nchunks + c, 0)),
            pl.BlockSpec((1, H_A, DK_A, DV_A), lambda b, c: (b, 0, 0, 0)),
        ],
        out_shape=[jax.ShapeDtypeStruct((rows_out, VA), BF16),
                   jax.ShapeDtypeStruct((nb, H_A, DK_A, DV_A), F32)],
        scratch_shapes=[pltpu.VMEM((H_A, DK_A, DV_A), F32)],
        compiler_params=_params("parallel", "arbitrary"),
        name="gla_mixer",
    )(P, P, P, P, LR, wlr, blr, gn, s0)


def _conv_kernel(ga_ref, gb_ref, halo_ref, cw_ref, cb_ref, lg_ref, lb_ref, o_ref, tail_ref, u_ref, *, R):
    @pl.when(pl.program_id(1) == 0)
    def _():
        u_ref[0:HALO] = halo_ref[0]

    u_ref[HALO:HALO + R] = ga_ref[...] * _sigmoid(gb_ref[...])
    acc = jnp.zeros((R, C_B), F32) + cb_ref[...]
    first = HALO - (CONV_W - 1)
    for w in range(CONV_W):
        acc = acc + u_ref[first + w:first + w + R, :] * cw_ref[w:w + 1, :]
    mu = jnp.mean(acc, axis=-1, keepdims=True)
    d = acc - mu
    var = jnp.mean(d * d, axis=-1, keepdims=True)
    y = d * lax.rsqrt(var + EPS) * lg_ref[...] + lb_ref[...]
    o_ref[...] = _silu(y).astype(o_ref.dtype)
    new_halo = u_ref[R:R + HALO]
    u_ref[0:HALO] = new_halo
    tail_ref[0] = new_halo


def conv_mixer(P, halo, cw, cb, lg, lb, nb, nchunks, R, row_block0, rows_out):
    rb = lambda b, c: row_block0 + b * nchunks + c
    ga_blk = (2 * QA + 2 * VA) // C_B
    return pl.pallas_call(
        functools.partial(_conv_kernel, R=R),
        grid=(nb, nchunks),
        in_specs=[
            pl.BlockSpec((R, C_B), lambda b, c: (rb(b, c), ga_blk)),
            pl.BlockSpec((R, C_B), lambda b, c: (rb(b, c), ga_blk + 1)),
            pl.BlockSpec((1, HALO, C_B), lambda b, c: (b, 0, 0)),
            pl.BlockSpec((HALO, C_B), lambda b, c: (0, 0)),
            pl.BlockSpec((1, C_B), lambda b, c: (0, 0)),
            pl.BlockSpec((1, C_B), lambda b, c: (0, 0)),
            pl.BlockSpec((1, C_B), lambda b, c: (0, 0)),
        ],
        out_specs=[
            pl.BlockSpec((R, C_B), lambda b, c: (b * nchunks + c, 0)),
            pl.BlockSpec((1, HALO, C_B), lambda b, c: (b, 0, 0)),
        ],
        out_shape=[jax.ShapeDtypeStruct((rows_out, C_B), BF16),
                   jax.ShapeDtypeStruct((nb, HALO, C_B), F32)],
        scratch_shapes=[pltpu.VMEM((HALO + R, C_B), F32)],
        compiler_params=_params("parallel", "arbitrary"),
        name="conv_mixer",
    )(P, P, halo, cw, cb, lg, lb)


def _sb_tiles(zs, Rs, vts, tri, mask):
    n = range(len(zs))
    sps = [_softplus(z) for z in zs]
    if mask is not None:
        sps = [jnp.where(mask, sp, 0.0) for sp in sps]
    his = [sp.astype(BF16) for sp in sps]
    los = [(sps[i] - his[i].astype(F32)).astype(BF16) for i in n]
    afts = [jnp.dot(his[i], tri, preferred_element_type=F32)
            + jnp.dot(los[i], tri, preferred_element_type=F32) + Rs[i] for i in n]
    aa = [jnp.exp(zs[i] - sps[i] - afts[i]) for i in n]
    if mask is not None:
        aa = [jnp.where(mask, a, 0.0) for a in aa]
    contribs = [jnp.dot(aa[i].astype(BF16), vts[i], preferred_element_type=F32) for i in n]
    return contribs, [afts[i][:, 0:1] + sps[i][:, 0:1] for i in n]


SB_HEADS_PER_STEP = 4


def _sb_prompt_kernel(bias_ref, q_ref, k_ref, v_ref, o_ref, *, L, TQ):
    M = N_META
    HPS = SB_HEADS_PER_STEP
    heads = range(HPS)
    hcol = [slice(i * DH_C, (i + 1) * DH_C) for i in heads]
    bias = [bias_ref[pl.program_id(1) * HPS + i] for i in heads]

    def tri_and_causal(n):
        row = lax.broadcasted_iota(jnp.int32, (n, n), 0)
        col = lax.broadcasted_iota(jnp.int32, (n, n), 1)
        return (row > col).astype(BF16), col < row

    def qrows(r0, n):
        return [(q_ref[r0:r0 + n, hcol[i]] * (DH_C ** -0.5)).astype(BF16) for i in heads]

    def tiles(qts, k0, n, Rs, tri, mask):
        zs = [lax.dot_general(qts[i], k_ref[pl.ds(k0, n), hcol[i]].astype(BF16), NT_DIMS,
                              preferred_element_type=F32) + bias[i] for i in heads]
        vts = [v_ref[pl.ds(k0, n), hcol[i]].astype(BF16) for i in heads]
        return _sb_tiles(zs, Rs, vts, tri, mask)

    tri_m, causal_m = tri_and_causal(M)
    tri_q, causal_q = tri_and_causal(TQ)

    accs, _ = tiles(qrows(0, M), 0, M, [jnp.zeros((M, 1), F32)] * HPS, tri_m, causal_m)
    for i in heads:
        o_ref[0:M, hcol[i]] = accs[i].astype(o_ref.dtype)

    for qi in range((L - M) // TQ):
        q0 = M + qi * TQ
        qts = qrows(q0, TQ)
        accs, Rs = tiles(qts, q0, TQ, [jnp.zeros((TQ, 1), F32)] * HPS, tri_q, causal_q)

        def body(j, carry, qts=qts, qi=qi):
            accs, Rs = carry
            k0 = pl.multiple_of(M + (qi - 1 - j) * TQ, SUBLANE)
            contribs, Rs = tiles(qts, k0, TQ, Rs, tri_q, None)
            return [a + c for a, c in zip(accs, contribs)], Rs

        if qi > 0:
            accs, Rs = lax.fori_loop(0, qi, body, (accs, Rs))
        contribs, _ = tiles(qts, 0, M, Rs, tri_m, None)
        for i in heads:
            o_ref[q0:q0 + TQ, hcol[i]] = (accs[i] + contribs[i]).astype(o_ref.dtype)


def sb_prompt_attn(QKV, bias, nb, L, TQ):
    HPS = SB_HEADS_PER_STEP
    W = HPS * DH_C
    nh = H_C // HPS
    return pl.pallas_call(
        functools.partial(_sb_prompt_kernel, L=L, TQ=TQ),
        grid=(nb, nh),
        in_specs=[
            pl.BlockSpec(memory_space=pltpu.SMEM),
            pl.BlockSpec((L, W), lambda b, h: (b, h)),
            pl.BlockSpec((L, W), lambda b, h: (b, nh + h)),
            pl.BlockSpec((L, W), lambda b, h: (b, 2 * nh + h)),
        ],
        out_specs=pl.BlockSpec((L, W), lambda b, h: (b, h)),
        out_shape=jax.ShapeDtypeStruct((nb * L, H_C * DH_C), BF16),
        compiler_params=_params("parallel", "parallel"),
        name="sb_prompt_attn",
    )(bias, QKV, QKV, QKV)


def _sb_sample_kernel(pt_ref, bias_ref, q_ref, kn_ref, vn_ref, kc_ref, vc_ref, o_ref,
                      qbd_ref, R_ref, acc_ref, kbuf, vbuf, sem, *, DS, n_steps, n_pages, PP, page0):
    b = pl.program_id(0)
    p = pl.program_id(1)
    HQ = H_C * DS
    D = H_C * DH_C
    NK = PP * PAGE_SIZE
    step = b * n_steps + p
    n_total = pl.num_programs(0) * n_steps

    def page_copies(g, slot):
        gb = g // n_steps
        gp = g - gb * n_steps
        cps = []
        for i in range(PP):
            page = page0 + pt_ref[gb, n_pages - PP * (gp + 1) + i]
            for c, (src, dst) in enumerate(((kc_ref, kbuf), (vc_ref, vbuf))):
                for h in range(H_C):
                    cps.append(pltpu.make_async_copy(src.at[page, :, h, :], dst.at[slot, i, h],
                                                     sem.at[c, slot]))
        return cps

    slot = step % 2

    @pl.when(step == 0)
    def _():
        for cp in page_copies(step, slot):
            cp.start()

    @pl.when(step + 1 < n_total)
    def _():
        for cp in page_copies(step + 1, 1 - slot):
            cp.start()

    def bias_row():
        lane = lax.broadcasted_iota(jnp.int32, (1, HQ), 1)
        out = jnp.zeros((1, HQ), F32)
        for h in range(H_C):
            out = jnp.where(lane // DS == h, bias_ref[h], out)
        return out

    def page_rows(buf, i):
        return jnp.concatenate([buf[slot, i, h] for h in range(H_C)], axis=1).astype(BF16)

    @pl.when(p == 0)
    def _():
        qs = q_ref[...] * (DH_C ** -0.5)
        eye = (lax.broadcasted_iota(jnp.int32, (DS, DS), 0)
               == lax.broadcasted_iota(jnp.int32, (DS, DS), 1)).astype(F32)
        qT = lax.dot_general(qs, eye, TN_DIMS, precision=HIGHEST, preferred_element_type=F32)
        rep = jnp.concatenate([qT] * H_C, axis=1)
        r_head = lax.broadcasted_iota(jnp.int32, (D, HQ), 0) // DH_C
        c_head = lax.broadcasted_iota(jnp.int32, (D, HQ), 1) // DS
        qbd = jnp.where(r_head == c_head, rep, 0.0).astype(BF16)
        qbd_ref[...] = qbd
        z = jnp.dot(kn_ref[...].astype(BF16), qbd, preferred_element_type=F32) + bias_row()
        jrow = lax.broadcasted_iota(jnp.int32, (DS, HQ), 0)
        tq = lax.broadcasted_iota(jnp.int32, (DS, HQ), 1) % DS
        mask = jrow < tq
        sp = jnp.where(mask, _softplus(z), 0.0)
        triu = (lax.broadcasted_iota(jnp.int32, (DS, DS), 1)
                > lax.broadcasted_iota(jnp.int32, (DS, DS), 0)).astype(F32)
        aft = jnp.dot(triu, sp, precision=HIGHEST, preferred_element_type=F32)
        a = jnp.where(mask, jnp.exp(z - sp - aft), 0.0)
        acc_ref[...] = lax.dot_general(a.astype(BF16), vn_ref[...].astype(BF16), TN_DIMS,
                                       preferred_element_type=F32)
        R_ref[...] = jnp.sum(sp, axis=0, keepdims=True)

    for cp in page_copies(step, slot):
        cp.wait()
    kp = jnp.concatenate([page_rows(kbuf, i) for i in range(PP)], axis=0)
    vp = jnp.concatenate([page_rows(vbuf, i) for i in range(PP)], axis=0)
    z = jnp.dot(kp, qbd_ref[...], preferred_element_type=F32) + bias_row()
    sp = _softplus(z)
    hi = sp.astype(BF16)
    lo = (sp - hi.astype(F32)).astype(BF16)
    triu = (lax.broadcasted_iota(jnp.int32, (NK, NK), 1)
            > lax.broadcasted_iota(jnp.int32, (NK, NK), 0)).astype(BF16)
    aft = (jnp.dot(triu, hi, preferred_element_type=F32)
           + jnp.dot(triu, lo, preferred_element_type=F32) + R_ref[...])
    a = jnp.exp(z - sp - aft)
    acc_ref[...] += lax.dot_general(a.astype(BF16), vp, TN_DIMS, preferred_element_type=F32)
    R_ref[...] = aft[0:1, :] + sp[0:1, :]

    @pl.when(p == n_steps - 1)
    def _():
        for h in range(H_C):
            o_ref[:, h * DH_C:(h + 1) * DH_C] = acc_ref[h * DS:(h + 1) * DS,
                                                        h * DH_C:(h + 1) * DH_C].astype(o_ref.dtype)


def sb_sample_attn(QKV, bias, cache_k, cache_v, layer, page_table, DB, DS, row_block0):
    n_layers, n_pool = cache_k.shape[:2]
    n_pages = page_table.shape[1]
    PP = 2 if n_pages % 2 == 0 else 1
    n_steps = n_pages // PP
    D = H_C * DH_C
    kc = cache_k.reshape(n_layers * n_pool, PAGE_SIZE, H_C, DH_C)
    vc = cache_v.reshape(n_layers * n_pool, PAGE_SIZE, H_C, DH_C)
    buf = pltpu.VMEM((2, PP, H_C, PAGE_SIZE, DH_C), F32)
    grid_spec = pltpu.PrefetchScalarGridSpec(
        num_scalar_prefetch=1,
        grid=(DB, n_steps),
        in_specs=[
            pl.BlockSpec(memory_space=pltpu.SMEM),
            pl.BlockSpec((DS, D), lambda b, p, pt: (row_block0 + b, 0)),
            pl.BlockSpec((DS, D), lambda b, p, pt: (row_block0 + b, 1)),
            pl.BlockSpec((DS, D), lambda b, p, pt: (row_block0 + b, 2)),
            pl.BlockSpec(memory_space=pl.ANY),
            pl.BlockSpec(memory_space=pl.ANY),
        ],
        out_specs=pl.BlockSpec((DS, D), lambda b, p, pt: (b, 0)),
        scratch_shapes=[pltpu.VMEM((D, H_C * DS), BF16),
                        pltpu.VMEM((1, H_C * DS), F32),
                        pltpu.VMEM((H_C * DS, D), F32),
                        buf, buf, pltpu.SemaphoreType.DMA((2, 2))],
    )
    return pl.pallas_call(
        functools.partial(_sb_sample_kernel, DS=DS, n_steps=n_steps, n_pages=n_pages, PP=PP,
                          page0=layer * n_pool),
        grid_spec=grid_spec,
        out_shape=jax.ShapeDtypeStruct((DB * DS, D), BF16),
        compiler_params=_params("arbitrary", "arbitrary"),
        name="sb_sample_attn",
    )(page_table, bias, QKV, QKV, QKV, kc, vc)


def _top_desc(x, n):
    rows = lax.broadcasted_iota(jnp.int32, x.shape, 0)
    big = jnp.int32(x.shape[0])
    vals = []
    for _ in range(n):
        m = jnp.max(x, axis=0, keepdims=True)
        first = jnp.min(jnp.where(x == m, rows, big), axis=0, keepdims=True)
        x = jnp.where(rows == first, -jnp.inf, x)
        vals.append(m)
    return vals


def _top_desc_distinct(x, n, n_pad):
    vals = []
    for _ in range(n):
        m = jnp.max(x, axis=0, keepdims=True)
        x = jnp.where(x == m, -jnp.inf, x)
        vals.append(m)
    dropped = jnp.sum(jnp.where(x == -jnp.inf, 1.0, 0.0), axis=0, keepdims=True)
    return vals, jnp.where(dropped != float(n + n_pad), 1.0, 0.0)


def _peer_route_kernel(q_ref, keys_ref, s1_ref, s2_ref, st_ref, *, TR):
    K = PEER_TOPK
    half = PEER_DQ // 2

    def run(h, exact):
        def top(x, n_pad):
            if exact:
                return _top_desc(x, K + 1), None
            return _top_desc_distinct(x, K + 1, n_pad)

        flags = jnp.zeros((1, TR), F32)
        tops = []
        for c, s_ref in enumerate((s1_ref, s2_ref)):
            if exact:
                s = s_ref[h]
            else:
                hc = 2 * h + c
                qh = q_ref[:, hc * half:(hc + 1) * half].astype(BF16)
                s = lax.dot_general(keys_ref[hc], qh, NT_DIMS, preferred_element_type=F32)
                s_ref[h] = s
            vals, bad = top(s, 0)
            tops.append(vals)
            if bad is not None:
                flags = jnp.maximum(flags, bad)
        v1, v2 = tops
        cand = []
        for a in range(K + 1):
            cand.extend(v1[a] + v2[b] for b in range((K + 1) // (a + 1)))
        pad = (-len(cand)) % SUBLANE
        cand = jnp.concatenate(cand + [jnp.full((pad, TR), -jnp.inf, F32)], axis=0)
        vals, bad = top(cand, pad)
        if bad is not None:
            flags = jnp.maximum(flags, bad)
        thr = 0.5 * (vals[K - 1] + vals[K])
        m = v1[0] + v2[0]
        zsum = jnp.sum(jnp.where(cand >= thr, jnp.exp(cand - m), 0.0), axis=0, keepdims=True)
        st_ref[h] = jnp.concatenate(
            [thr, v1[0], v2[0], 1.0 / zsum, jnp.zeros((4, TR), F32)], axis=0)
        return flags

    for h in range(PEER_HEADS):
        flags = run(h, exact=False)

        @pl.when(jnp.max(flags) > 0.0)
        def _(h=h):
            run(h, exact=True)


def peer_route(q, keys_bf16, TR):
    T = q.shape[0]
    s_spec = pl.BlockSpec((PEER_HEADS, PEER_NKEYS, TR), lambda i: (0, 0, i))
    s_shape = jax.ShapeDtypeStruct((PEER_HEADS, PEER_NKEYS, T), F32)
    return pl.pallas_call(
        functools.partial(_peer_route_kernel, TR=TR),
        grid=(T // TR,),
        in_specs=[pl.BlockSpec((TR, PEER_HEADS * PEER_DQ), lambda i: (i, 0)),
                  pl.BlockSpec((2 * PEER_HEADS, PEER_NKEYS, PEER_DQ // 2), lambda i: (0, 0, 0))],
        out_specs=[s_spec, s_spec, pl.BlockSpec((PEER_HEADS, 8, TR), lambda i: (0, 0, i))],
        out_shape=[s_shape, s_shape, jax.ShapeDtypeStruct((PEER_HEADS, 8, T), F32)],
        compiler_params=_params("parallel"),
        name="peer_route",
    )(q, keys_bf16)


GATE_ROWS = 32
PEER_I1_BLOCK = SUBLANE
GELU_C0 = 0.7978845608028654
GELU_C1 = 0.7978845608028654 * 0.044715


def _peer_dense_kernel(h_ref, u_ref, vt_ref, s1_ref, s2_ref, st_ref, o_ref, e2_ref,
                       a0_ref, a1_ref, p0_ref, p1_ref, *, R, TB):
    j = pl.program_id(1)

    @pl.when(j == 0)
    def _():
        o_ref[...] = jnp.zeros_like(o_ref)
        for h in range(PEER_HEADS):
            e2_ref[h] = jnp.exp(s2_ref[h] - st_ref[h, 2:3, :]) * (0.5 * st_ref[h, 3:4, :])

    RH = R // 2
    acc = None
    for hf, (a_ref, p_ref) in enumerate(((a0_ref, p0_ref), (a1_ref, p1_ref))):
        e0 = hf * RH * PEER_NKEYS
        a_ref[...] = lax.dot_general(u_ref[e0:e0 + RH * PEER_NKEYS, :], h_ref[...], NT_DIMS,
                                     preferred_element_type=F32)
        for r in range(RH):
            rr = hf * RH + r
            for lc in range(TB // LANE):
                lanes = slice(lc * LANE, (lc + 1) * LANE)
                rowv = []
                for h in range(PEER_HEADS):
                    s1 = s1_ref[h, rr:rr + 1, lanes]
                    rowv.append((st_ref[h, 0:1, lanes] - s1, jnp.exp(s1 - st_ref[h, 1:2, lanes])))
                for g in range(PEER_NKEYS // GATE_ROWS):
                    sub = slice(g * GATE_ROWS, (g + 1) * GATE_ROWS)
                    gate = jnp.zeros((GATE_ROWS, LANE), F32)
                    for h in range(PEER_HEADS):
                        need, e1 = rowv[h]
                        gate = gate + jnp.where(s2_ref[h, sub, lanes] >= need, e1 * e2_ref[h, sub, lanes], 0.0)
                    rows = slice(r * PEER_NKEYS + g * GATE_ROWS, r * PEER_NKEYS + (g + 1) * GATE_ROWS)
                    a = a_ref[rows, lanes]
                    t = jnp.tanh(a * (GELU_C0 + GELU_C1 * (a * a)))
                    p_ref[rows, lanes] = (gate * (a + a * t)).astype(BF16)
        part = jnp.dot(vt_ref[:, e0:e0 + RH * PEER_NKEYS], p_ref[...], preferred_element_type=F32)
        acc = part if acc is None else acc + part
    o_ref[...] += acc


def peer_dense(h, u_bf16, vt_bf16, s1, s2, st, TB):
    T, D = h.shape
    R = PEER_I1_BLOCK
    half_shape = (R // 2 * PEER_NKEYS, TB)
    return pl.pallas_call(
        functools.partial(_peer_dense_kernel, R=R, TB=TB),
        grid=(T // TB, PEER_NKEYS // R),
        in_specs=[pl.BlockSpec((TB, D), lambda i, j: (i, 0)),
                  pl.BlockSpec((R * PEER_NKEYS, D), lambda i, j: (j, 0)),
                  pl.BlockSpec((D, R * PEER_NKEYS), lambda i, j: (0, j)),
                  pl.BlockSpec((PEER_HEADS, R, TB), lambda i, j: (0, j, i)),
                  pl.BlockSpec((PEER_HEADS, PEER_NKEYS, TB), lambda i, j: (0, 0, i)),
                  pl.BlockSpec((PEER_HEADS, 8, TB), lambda i, j: (0, 0, i))],
        out_specs=pl.BlockSpec((D, TB), lambda i, j: (0, i)),
        out_shape=jax.ShapeDtypeStruct((D, T), F32),
        scratch_shapes=[pltpu.VMEM((PEER_HEADS, PEER_NKEYS, TB), F32),
                        pltpu.VMEM(half_shape, F32), pltpu.VMEM(half_shape, F32),
                        pltpu.VMEM(half_shape, BF16), pltpu.VMEM(half_shape, BF16)],
        compiler_params=_params("parallel", "arbitrary"),
        name="peer_dense",
    )(h, u_bf16, vt_bf16, s1, s2, st)


def peer_layer(x, g, wq, keys, u, v, TB):
    q, hn = rms_matmul(x, g, wq.astype(BF16), TB, 512, emit_h=True)
    keys_b = keys.reshape(2 * PEER_HEADS, PEER_NKEYS, PEER_DQ // 2).astype(BF16)
    s1, s2, st = peer_route(q, keys_b, 256)
    yt = peer_dense(hn, u.astype(BF16), v.T.astype(BF16), s1, s2, st, TB)
    return transpose_add(x, yt, TB)


def _pick_tile(n, cap, mult):
    best = mult
    for t in range(mult, cap + 1, mult):
        if n % t == 0:
            best = t
    return best


def kernel(x_prompt, x_sample, state_gla, state_conv, cache_k, cache_v, page_table, meta_tokens, norm_mix, norm_ffn, norm_final, w_in_even, w_gate_lr, b_gate_lr, gla_norm, conv_w, conv_b, conv_norm_g, conv_norm_b, w_out_even, w_qkv_odd, w_out_odd, sb_bias, peer_wq, peer_keys, peer_u, peer_v):
    B, SEQ, D = x_prompt.shape
    DB, DS, _ = x_sample.shape
    depth = norm_mix.shape[0]
    L = N_META + SEQ
    Tp, Ts = B * L, DB * DS
    T = Tp + Ts
    TB = 768 if T >= 768 else 256
    Tpad = -(-T // TB) * TB
    GC = 48 if L % 48 == 0 else 16
    assert L % GC == 0 and Tp % DS == 0 and DS % SUBLANE == 0 and DS <= 16
    TQ = _pick_tile(SEQ, 256, LANE)
    assert SEQ % TQ == 0

    xp = jnp.concatenate([jnp.broadcast_to(meta_tokens[None], (B, N_META, D)), x_prompt], axis=1)
    X = jnp.concatenate([xp.reshape(Tp, D), x_sample.reshape(Ts, D), jnp.zeros((Tpad - T, D), F32)], axis=0)

    gla_p, gla_s, conv_p, conv_s, k_p, v_p, k_s, v_s = [], [], [], [], [], [], [], []
    for layer in range(depth):
        i = layer // 2
        if layer % 2 == 0:
            w = w_in_even[i]
            n_main = 2 * QA + 2 * VA
            w_main = jnp.concatenate([w[:, :n_main], w[:, n_main + LOWRANK_A:]], axis=1).astype(BF16)
            w_lr = jnp.pad(w[:, n_main:n_main + LOWRANK_A], ((0, 0), (0, LANE - LOWRANK_A))).astype(BF16)
            P = rms_matmul(X, norm_mix[layer], w_main, TB, 512)
            LR = rms_matmul(X, norm_mix[layer], w_lr, TB, LANE)
            wlr = jnp.pad(w_gate_lr[i], ((0, LANE - LOWRANK_A), (0, 0))).astype(BF16)
            blr = b_gate_lr[i].reshape(1, QA)
            gn = gla_norm[i].reshape(1, DV_A)
            oa_p, sp_ = gla_mixer(P, LR, wlr, blr, gn, jnp.zeros((B, H_A, DK_A, DV_A), F32),
                                  B, L // GC, GC, 0, Tp)
            oa_s, ss_ = gla_mixer(P, LR, wlr, blr, gn, state_gla[i], DB, 1, DS, Tp // DS, Ts)
            cw = jnp.pad(conv_w[i], ((0, HALO - CONV_W), (0, 0)))
            cvec = [t.reshape(1, C_B) for t in (conv_b[i], conv_norm_g[i], conv_norm_b[i])]
            ob_p, tail_p = conv_mixer(P, jnp.zeros((B, HALO, C_B), F32), cw, *cvec, B, L // GC, GC, 0, Tp)
            halo_s = jnp.pad(state_conv[i], ((0, 0), (HALO - (CONV_W - 1), 0), (0, 0)))
            ob_s, tail_s = conv_mixer(P, halo_s, cw, *cvec, DB, 1, DS, Tp // DS, Ts)
            mix = jnp.concatenate([jnp.concatenate([oa_p, ob_p], axis=1),
                                   jnp.concatenate([oa_s, ob_s], axis=1),
                                   jnp.zeros((Tpad - T, VA + C_B), BF16)], axis=0)
            X = matmul_res(mix, w_out_even[i].astype(BF16), X, TB, 512)
            gla_p.append(sp_)
            gla_s.append(ss_)
            conv_p.append(tail_p[:, HALO - (CONV_W - 1):])
            conv_s.append(tail_s[:, HALO - (CONV_W - 1):])
        else:
            QKV = rms_matmul(X, norm_mix[layer], w_qkv_odd[i].astype(BF16), TB, 512)
            HD = H_C * DH_C
            o_p = sb_prompt_attn(QKV, sb_bias[i], B, L, TQ)
            o_s = sb_sample_attn(QKV, sb_bias[i], cache_k, cache_v, i, page_table, DB, DS, Tp // DS)
            att = jnp.concatenate([o_p, o_s, jnp.zeros((Tpad - T, HD), BF16)], axis=0)
            X = matmul_res(att, w_out_odd[i].astype(BF16), X, TB, 512)
            k_p.append(QKV[:Tp, HD:2 * HD].reshape(B, L, H_C, DH_C))
            v_p.append(QKV[:Tp, 2 * HD:].reshape(B, L, H_C, DH_C))
            k_s.append(QKV[Tp:T, HD:2 * HD].reshape(DB, DS, H_C, DH_C))
            v_s.append(QKV[Tp:T, 2 * HD:].reshape(DB, DS, H_C, DH_C))
        X = peer_layer(X, norm_ffn[layer], peer_wq[layer], peer_keys[layer], peer_u[layer], peer_v[layer], TB)

    Y = rmsnorm_rows(X, norm_final, TB)
    y_prompt = Y[:Tp].reshape(B, L, D)[:, N_META:]
    y_sample = Y[Tp:T].reshape(DB, DS, D)
    return (y_prompt, y_sample, jnp.stack(gla_p), jnp.stack(gla_s), jnp.stack(conv_p), jnp.stack(conv_s),
            jnp.stack(k_p), jnp.stack(v_p), jnp.stack(k_s), jnp.stack(v_s))
```

```python
import functools

import jax
import jax.numpy as jnp
from jax import lax
from jax.experimental import pallas as pl
from jax.experimental.pallas import tpu as pltpu

F32 = jnp.float32
BF16 = jnp.bfloat16
HIGHEST = lax.Precision.HIGHEST

N_META = 16
EPS = 1e-6
H_A = 4
DK_A = 128
DV_A = 256
LOWRANK_A = 16
GATE_TAU = 16.0
C_B = 1024
CONV_W = 31
H_C = 16
DH_C = 128
PAGE_SIZE = 128
PEER_HEADS = 8
PEER_NKEYS = 128
PEER_DQ = 256
PEER_TOPK = 16
QA = H_A * DK_A
VA = H_A * DV_A

LANE = 128
SUBLANE = 8
HALO = 32
VMEM_LIMIT = 56 * 1024 * 1024

NT_DIMS = (((1,), (1,)), ((), ()))
TN_DIMS = (((0,), (0,)), ((), ()))


def _params(*sem):
    return pltpu.CompilerParams(dimension_semantics=sem, vmem_limit_bytes=VMEM_LIMIT)


def _softplus(x):
    return jnp.maximum(x, 0.0) + jnp.log(1.0 + jnp.exp(-jnp.abs(x)))


def _log_sigmoid(x):
    return -_softplus(-x)


def _sigmoid(x):
    return 1.0 / (1.0 + jnp.exp(-x))


def _silu(x):
    return x * _sigmoid(x)


def _gelu_tanh(x):
    c = 0.7978845608028654
    return 0.5 * x * (1.0 + jnp.tanh(c * (x + 0.044715 * (x * x * x))))


def _rms_matmul_kernel(x_ref, g_ref, w_ref, o_ref, *rest, emit_h):
    h_ref = rest[-1]

    @pl.when(pl.program_id(1) == 0)
    def _():
        x = x_ref[...]
        ms = jnp.mean(x * x, axis=-1, keepdims=True)
        h = (x * lax.rsqrt(ms + EPS) * g_ref[...]).astype(BF16)
        h_ref[...] = h
        if emit_h:
            rest[0][...] = h

    o_ref[...] = jnp.dot(h_ref[...], w_ref[...], preferred_element_type=F32)


def rms_matmul(x, g, w, tm, tn, emit_h=False):
    T, D = x.shape
    N = w.shape[1]
    out_shape = [jax.ShapeDtypeStruct((T, N), F32)]
    out_specs = [pl.BlockSpec((tm, tn), lambda i, j: (i, j))]
    if emit_h:
        out_shape.append(jax.ShapeDtypeStruct((T, D), BF16))
        out_specs.append(pl.BlockSpec((tm, D), lambda i, j: (i, 0)))
    res = pl.pallas_call(
        functools.partial(_rms_matmul_kernel, emit_h=emit_h),
        grid=(T // tm, N // tn),
        in_specs=[pl.BlockSpec((tm, D), lambda i, j: (i, 0)),
                  pl.BlockSpec((1, D), lambda i, j: (0, 0)),
                  pl.BlockSpec((D, tn), lambda i, j: (0, j))],
        out_specs=out_specs,
        out_shape=out_shape,
        scratch_shapes=[pltpu.VMEM((tm, D), BF16)],
        compiler_params=_params("parallel", "arbitrary"),
        name="rms_matmul",
    )(x, g.reshape(1, D), w)
    return res if emit_h else res[0]


def _matmul_res_kernel(*refs, widths):
    a_refs, (w_ref, r_ref, o_ref) = refs[:len(widths)], refs[len(widths):]
    acc = r_ref[...]
    k0 = 0
    for a_ref, kw in zip(a_refs, widths):
        acc = acc + jnp.dot(a_ref[...], w_ref[k0:k0 + kw, :], preferred_element_type=F32)
        k0 += kw
    o_ref[...] = acc


def matmul_res(parts, w, res, tm, tn):
    T = parts[0].shape[0]
    widths = tuple(a.shape[1] for a in parts)
    K, N = w.shape
    assert sum(widths) == K
    return pl.pallas_call(
        functools.partial(_matmul_res_kernel, widths=widths),
        grid=(T // tm, N // tn),
        in_specs=[pl.BlockSpec((tm, kw), lambda i, j: (i, 0)) for kw in widths]
        + [pl.BlockSpec((K, tn), lambda i, j: (0, j)),
           pl.BlockSpec((tm, tn), lambda i, j: (i, j))],
        out_specs=pl.BlockSpec((tm, tn), lambda i, j: (i, j)),
        out_shape=jax.ShapeDtypeStruct((T, N), F32),
        compiler_params=_params("parallel", "arbitrary"),
        name="matmul_res",
    )(*parts, w, res)


def _rms_qkv_kernel(x_ref, g_ref, w_ref, q_ref, k_ref, v_ref, h_ref, *, nq):
    j = pl.program_id(1)

    @pl.when(j == 0)
    def _():
        x = x_ref[...]
        ms = jnp.mean(x * x, axis=-1, keepdims=True)
        h_ref[...] = (x * lax.rsqrt(ms + EPS) * g_ref[...]).astype(BF16)

    res = jnp.dot(h_ref[...], w_ref[...], preferred_element_type=F32)

    @pl.when(j < nq)
    def _():
        q_ref[...] = res

    @pl.when((j >= nq) & (j < 2 * nq))
    def _():
        k_ref[...] = res

    @pl.when(j >= 2 * nq)
    def _():
        v_ref[...] = res


def rms_qkv(x, g, w, tm, tn, rows_kv):
    T, D = x.shape
    HD = w.shape[1] // 3
    nq = HD // tn
    clip = lambda j, part: jnp.clip(j - part * nq, 0, nq - 1)
    return pl.pallas_call(
        functools.partial(_rms_qkv_kernel, nq=nq),
        grid=(T // tm, 3 * nq),
        in_specs=[pl.BlockSpec((tm, D), lambda i, j: (i, 0)),
                  pl.BlockSpec((1, D), lambda i, j: (0, 0)),
                  pl.BlockSpec((D, tn), lambda i, j: (0, j))],
        out_specs=[pl.BlockSpec((tm, tn), lambda i, j: (i, clip(j, 0))),
                   pl.BlockSpec((tm, tn), lambda i, j: (i, clip(j, 1))),
                   pl.BlockSpec((tm, tn), lambda i, j: (i, clip(j, 2)))],
        out_shape=[jax.ShapeDtypeStruct((T, HD), F32),
                   jax.ShapeDtypeStruct((rows_kv, HD), F32),
                   jax.ShapeDtypeStruct((rows_kv, HD), F32)],
        scratch_shapes=[pltpu.VMEM((tm, D), BF16)],
        compiler_params=_params("parallel", "arbitrary"),
        name="rms_qkv",
    )(x, g.reshape(1, D), w)


def _rmsnorm_kernel(x_ref, g_ref, o_ref):
    x = x_ref[...]
    ms = jnp.mean(x * x, axis=-1, keepdims=True)
    o_ref[...] = (x * lax.rsqrt(ms + EPS) * g_ref[...]).reshape(o_ref.shape)


def rmsnorm_seqs(x, g, nb, seq_len, row0, row_stride, tr):
    D = x.shape[1]
    rows = lambda b, j: pl.multiple_of(row0 + b * row_stride + j * tr, SUBLANE)
    return pl.pallas_call(
        _rmsnorm_kernel,
        grid=(nb, seq_len // tr),
        in_specs=[pl.BlockSpec((pl.Element(tr), pl.Element(D)), lambda b, j: (rows(b, j), 0)),
                  pl.BlockSpec((1, D), lambda b, j: (0, 0))],
        out_specs=pl.BlockSpec((1, tr, D), lambda b, j: (b, j, 0)),
        out_shape=jax.ShapeDtypeStruct((nb, seq_len, D), F32),
        compiler_params=_params("parallel", "parallel"),
        name="rmsnorm_seqs",
    )(x, g.reshape(1, D))


def rmsnorm_rows(x, g, tm):
    T, D = x.shape
    return pl.pallas_call(
        _rmsnorm_kernel,
        grid=(T // tm,),
        in_specs=[pl.BlockSpec((tm, D), lambda i: (i, 0)),
                  pl.BlockSpec((1, D), lambda i: (0, 0))],
        out_specs=pl.BlockSpec((tm, D), lambda i: (i, 0)),
        out_shape=jax.ShapeDtypeStruct((T, D), F32),
        compiler_params=_params("parallel"),
        name="rmsnorm_final",
    )(x, g.reshape(1, D))


def _transpose_cast_kernel(x_ref, o_ref):
    o_ref[...] = x_ref[...].T.astype(o_ref.dtype)


def transpose_cast(xs, layer, tr, dtype):
    n_layers, N, D = xs.shape
    nblk = N // tr
    return pl.pallas_call(
        _transpose_cast_kernel,
        grid=(nblk,),
        in_specs=[pl.BlockSpec((tr, D), lambda i: (layer * nblk + i, 0))],
        out_specs=pl.BlockSpec((D, tr), lambda i: (0, i)),
        out_shape=jax.ShapeDtypeStruct((D, N), dtype),
        compiler_params=_params("parallel"),
        name="transpose_cast",
    )(xs.reshape(n_layers * N, D))


def _transpose_add_kernel(x_ref, yt_ref, o_ref):
    o_ref[...] = x_ref[...] + yt_ref[...].T


def transpose_add(x, yt, tb):
    T, D = x.shape
    return pl.pallas_call(
        _transpose_add_kernel,
        grid=(T // tb,),
        in_specs=[pl.BlockSpec((tb, D), lambda i: (i, 0)),
                  pl.BlockSpec((D, tb), lambda i: (0, i))],
        out_specs=pl.BlockSpec((tb, D), lambda i: (i, 0)),
        out_shape=jax.ShapeDtypeStruct((T, D), F32),
        compiler_params=_params("parallel"),
        name="transpose_add",
    )(x, yt)


def _gla_kernel(q_ref, k_ref, v_ref, g_ref, lr_ref, wlr_ref, blr_ref, gn_ref, s0_ref, dst_ref,
                o_ref, sout_ref, S_ref, *, C, sb):
    del dst_ref
    c = pl.program_id(1)

    @pl.when(c == 0)
    def _():
        S_ref[...] = s0_ref[0]

    x = jnp.dot(lr_ref[...].astype(BF16), wlr_ref[...], preferred_element_type=F32) + blr_ref[...]
    loga_all = _log_sigmoid(x) * (1.0 / GATE_TAU)
    row = lax.broadcasted_iota(jnp.int32, (C, C), 0)
    col = lax.broadcasted_iota(jnp.int32, (C, C), 1)
    tril = (row >= col).astype(F32)
    b_all = jnp.dot(tril, loga_all, precision=HIGHEST, preferred_element_type=F32)
    H = range(H_A)
    kk = [slice(h * DK_A, (h + 1) * DK_A) for h in H]
    vv = [slice(h * DV_A, (h + 1) * DV_A) for h in H]
    q = [q_ref[:, kk[h]] * (DK_A ** -0.5) for h in H]
    k = [k_ref[:, kk[h]] for h in H]
    v = [v_ref[:, vv[h]] for h in H]
    b = [b_all[:, kk[h]] for h in H]
    S = [S_ref[h] for h in H]
    hp = dict(precision=HIGHEST, preferred_element_type=F32)
    lp = dict(preferred_element_type=F32)

    o = [jnp.dot((q[h] * jnp.exp(b[h])).astype(BF16), S[h].astype(BF16), **lp) for h in H]
    tpos = lax.broadcasted_iota(jnp.int32, (sb, 1), 0)
    outs = [[] for _ in H]
    for i in range(C // sb):
        r0 = i * sb
        rows = slice(r0, r0 + sb)
        oi = [o[h][rows] for h in H]
        if i > 0:
            qt = [q[h][rows] * jnp.exp(b[h][rows] - b[h][r0 - 1:r0]) for h in H]
            kt = [k[h][:r0] * jnp.exp(b[h][r0 - 1:r0] - b[h][:r0]) for h in H]
            att = [lax.dot_general(qt[h].astype(BF16), kt[h].astype(BF16), NT_DIMS, **lp) for h in H]
            oi = [oi[h] + jnp.dot(att[h].astype(BF16), v[h][:r0].astype(BF16), **lp) for h in H]
        for s in range(sb):
            for h in H:
                bi = b[h][rows]
                dec = jnp.exp(jnp.minimum(bi - bi[s:s + 1], 0.0))
                a_col = jnp.sum(q[h][rows] * k[h][r0 + s:r0 + s + 1] * dec, axis=-1, keepdims=True)
                a_col = jnp.where(tpos >= s, a_col, 0.0)
                oi[h] = oi[h] + a_col * v[h][r0 + s:r0 + s + 1]
        for h in H:
            outs[h].append(oi[h])

    for h in H:
        oh = outs[h][0] if len(outs[h]) == 1 else jnp.concatenate(outs[h], axis=0)
        ms = jnp.mean(oh * oh, axis=-1, keepdims=True)
        on = oh * lax.rsqrt(ms + EPS) * gn_ref[...]
        o_ref[:, vv[h]] = (on * _silu(g_ref[:, vv[h]])).astype(o_ref.dtype)

    ones = jnp.ones((C, DV_A), F32)
    kd = [k[h] * jnp.exp(b[h][C - 1:C] - b[h]) for h in H]
    tot = [lax.dot_general(loga_all[:, kk[h]], ones, TN_DIMS, **hp) for h in H]
    upd = [lax.dot_general(kd[h], v[h], TN_DIMS, **hp) for h in H]
    for h in H:
        S_new = jnp.exp(tot[h]) * S[h] + upd[h]
        S_ref[h] = S_new
        sout_ref[0, h] = S_new


def gla_mixer(P, LR, wlr, blr, gn, s0, dst, nb, nchunks, C, row_block0):
    sb = min(16, C)
    rb = lambda b, c: row_block0 + b * nchunks + c
    return pl.pallas_call(
        functools.partial(_gla_kernel, C=C, sb=sb),
        grid=(nb, nchunks),
        in_specs=[
            pl.BlockSpec((C, QA), lambda b, c: (rb(b, c), 0)),
            pl.BlockSpec((C, QA), lambda b, c: (rb(b, c), 1)),
            pl.BlockSpec((C, VA), lambda b, c: (rb(b, c), 2 * QA // VA)),
            pl.BlockSpec((C, VA), lambda b, c: (rb(b, c), 2 * QA // VA + 1)),
            pl.BlockSpec((C, LANE), lambda b, c: (rb(b, c), 0)),
            pl.BlockSpec((LANE, QA), lambda b, c: (0, 0)),
            pl.BlockSpec((1, QA), lambda b, c: (0, 0)),
            pl.BlockSpec((1, DV_A), lambda b, c: (0, 0)),
            pl.BlockSpec((1, H_A, DK_A, DV_A), lambda b, c: (b, 0, 0, 0)),
            pl.BlockSpec(memory_space=pl.ANY),
        ],
        out_specs=[
            pl.BlockSpec((C, VA), lambda b, c: (rb(b, c), 0)),
            pl.BlockSpec((1, H_A, DK_A, DV_A), lambda b, c: (b, 0, 0, 0)),
        ],
        out_shape=[jax.ShapeDtypeStruct(dst.shape, dst.dtype),
                   jax.ShapeDtypeStruct((nb, H_A, DK_A, DV_A), F32)],
        scratch_shapes=[pltpu.VMEM((H_A, DK_A, DV_A), F32)],
        input_output_aliases={9: 0},
        compiler_params=_params("parallel", "arbitrary"),
        name="gla_mixer",
    )(P, P, P, P, LR, wlr, blr, gn, s0, dst)


def _conv_kernel(ga_ref, gb_ref, halo_ref, cw_ref, cb_ref, lg_ref, lb_ref, dst_ref, o_ref, tail_ref,
                 u_ref, sh_ref, *, R):
    del dst_ref

    @pl.when(pl.program_id(1) == 0)
    def _():
        u_ref[0:HALO] = halo_ref[0]

    u_ref[HALO:HALO + R] = ga_ref[...] * _sigmoid(gb_ref[...])
    span = HALO + R - SUBLANE
    for s in range(1, SUBLANE):
        sh_ref[s - 1, 0:span] = u_ref[s:s + span]
    acc = jnp.zeros((R, C_B), F32) + cb_ref[...]
    first = HALO - (CONV_W - 1)
    for w in range(CONV_W):
        a8, s = (first + w) // SUBLANE * SUBLANE, (first + w) % SUBLANE
        rows = u_ref[a8:a8 + R, :] if s == 0 else sh_ref[s - 1, a8:a8 + R, :]
        acc = acc + rows * cw_ref[w:w + 1, :]
    mu = jnp.mean(acc, axis=-1, keepdims=True)
    d = acc - mu
    var = jnp.mean(d * d, axis=-1, keepdims=True)
    y = d * lax.rsqrt(var + EPS) * lg_ref[...] + lb_ref[...]
    o_ref[...] = _silu(y).astype(o_ref.dtype)
    new_halo = u_ref[R:R + HALO]
    u_ref[0:HALO] = new_halo
    tail_ref[0] = new_halo


def conv_mixer(P, halo, cw, cb, lg, lb, dst, nb, nchunks, R, row_block0):
    rb = lambda b, c: row_block0 + b * nchunks + c
    ga_blk = (2 * QA + 2 * VA) // C_B
    return pl.pallas_call(
        functools.partial(_conv_kernel, R=R),
        grid=(nb, nchunks),
        in_specs=[
            pl.BlockSpec((R, C_B), lambda b, c: (rb(b, c), ga_blk)),
            pl.BlockSpec((R, C_B), lambda b, c: (rb(b, c), ga_blk + 1)),
            pl.BlockSpec((1, HALO, C_B), lambda b, c: (b, 0, 0)),
            pl.BlockSpec((HALO, C_B), lambda b, c: (0, 0)),
            pl.BlockSpec((1, C_B), lambda b, c: (0, 0)),
            pl.BlockSpec((1, C_B), lambda b, c: (0, 0)),
            pl.BlockSpec((1, C_B), lambda b, c: (0, 0)),
            pl.BlockSpec(memory_space=pl.ANY),
        ],
        out_specs=[
            pl.BlockSpec((R, C_B), lambda b, c: (rb(b, c), 0)),
            pl.BlockSpec((1, HALO, C_B), lambda b, c: (b, 0, 0)),
        ],
        out_shape=[jax.ShapeDtypeStruct(dst.shape, dst.dtype),
                   jax.ShapeDtypeStruct((nb, HALO, C_B), F32)],
        scratch_shapes=[pltpu.VMEM((HALO + R, C_B), F32),
                        pltpu.VMEM((SUBLANE - 1, HALO + R, C_B), F32)],
        input_output_aliases={7: 0},
        compiler_params=_params("parallel", "arbitrary"),
        name="conv_mixer",
    )(P, P, halo, cw, cb, lg, lb, dst)


def _sb_tiles(zs, Rs, vts, tri, mask):
    n = range(len(zs))
    sps = [_softplus(z) for z in zs]
    if mask is not None:
        sps = [jnp.where(mask, sp, 0.0) for sp in sps]
    his = [sp.astype(BF16) for sp in sps]
    los = [(sps[i] - his[i].astype(F32)).astype(BF16) for i in n]
    afts = [jnp.dot(his[i], tri, preferred_element_type=F32)
            + jnp.dot(los[i], tri, preferred_element_type=F32) + Rs[i] for i in n]
    aa = [jnp.exp(zs[i] - sps[i] - afts[i]) for i in n]
    if mask is not None:
        aa = [jnp.where(mask, a, 0.0) for a in aa]
    contribs = [jnp.dot(aa[i].astype(BF16), vts[i], preferred_element_type=F32) for i in n]
    return contribs, [afts[i][:, 0:1] + sps[i][:, 0:1] for i in n]


SB_HEADS_PER_STEP = 4


def _sb_prompt_kernel(bias_ref, q_ref, k_ref, v_ref, dst_ref, o_ref, *, L, TQ):
    del dst_ref
    M = N_META
    HPS = SB_HEADS_PER_STEP
    heads = range(HPS)
    hcol = [slice(i * DH_C, (i + 1) * DH_C) for i in heads]
    bias = [bias_ref[pl.program_id(1) * HPS + i] for i in heads]

    def tri_and_causal(n):
        row = lax.broadcasted_iota(jnp.int32, (n, n), 0)
        col = lax.broadcasted_iota(jnp.int32, (n, n), 1)
        return (row > col).astype(BF16), col < row

    def qrows(r0, n):
        return [(q_ref[r0:r0 + n, hcol[i]] * (DH_C ** -0.5)).astype(BF16) for i in heads]

    def tiles(qts, k0, n, Rs, tri, mask):
        zs = [lax.dot_general(qts[i], k_ref[pl.ds(k0, n), hcol[i]].astype(BF16), NT_DIMS,
                              preferred_element_type=F32) + bias[i] for i in heads]
        vts = [v_ref[pl.ds(k0, n), hcol[i]].astype(BF16) for i in heads]
        return _sb_tiles(zs, Rs, vts, tri, mask)

    tri_m, causal_m = tri_and_causal(M)
    tri_q, causal_q = tri_and_causal(TQ)

    accs, _ = tiles(qrows(0, M), 0, M, [jnp.zeros((M, 1), F32)] * HPS, tri_m, causal_m)
    for i in heads:
        o_ref[0:M, hcol[i]] = accs[i].astype(o_ref.dtype)

    for qi in range((L - M) // TQ):
        q0 = M + qi * TQ
        qts = qrows(q0, TQ)
        accs, Rs = tiles(qts, q0, TQ, [jnp.zeros((TQ, 1), F32)] * HPS, tri_q, causal_q)

        def body(j, carry, qts=qts, qi=qi):
            accs, Rs = carry
            k0 = pl.multiple_of(M + (qi - 1 - j) * TQ, SUBLANE)
            contribs, Rs = tiles(qts, k0, TQ, Rs, tri_q, None)
            return [a + c for a, c in zip(accs, contribs)], Rs

        if qi > 0:
            accs, Rs = lax.fori_loop(0, qi, body, (accs, Rs))
        contribs, _ = tiles(qts, 0, M, Rs, tri_m, None)
        for i in heads:
            o_ref[q0:q0 + TQ, hcol[i]] = (accs[i] + contribs[i]).astype(o_ref.dtype)


def sb_prompt_attn(q, k, v, bias, dst, nb, L, TQ):
    HPS = SB_HEADS_PER_STEP
    W = HPS * DH_C
    blk = pl.BlockSpec((L, W), lambda b, h: (b, h))
    return pl.pallas_call(
        functools.partial(_sb_prompt_kernel, L=L, TQ=TQ),
        grid=(nb, H_C // HPS),
        in_specs=[pl.BlockSpec(memory_space=pltpu.SMEM), blk, blk, blk, pl.BlockSpec(memory_space=pl.ANY)],
        out_specs=blk,
        out_shape=jax.ShapeDtypeStruct(dst.shape, dst.dtype),
        input_output_aliases={4: 0},
        compiler_params=_params("parallel", "parallel"),
        name="sb_prompt_attn",
    )(bias, q, k, v, dst)


def _sb_sample_kernel(pt_ref, bias_ref, q_ref, kn_ref, vn_ref, kc_ref, vc_ref, dst_ref, o_ref,
                      qbd_ref, R_ref, acc_ref, kbuf, vbuf, sem, *, DS, n_steps, n_pages, PP, page0):
    del dst_ref
    b = pl.program_id(0)
    p = pl.program_id(1)
    HQ = H_C * DS
    D = H_C * DH_C
    NK = PP * PAGE_SIZE
    step = b * n_steps + p
    n_total = pl.num_programs(0) * n_steps

    def page_copies(g, slot):
        gb = g // n_steps
        gp = g - gb * n_steps
        cps = []
        for i in range(PP):
            page = page0 + pt_ref[gb, n_pages - PP * (gp + 1) + i]
            for c, (src, dst) in enumerate(((kc_ref, kbuf), (vc_ref, vbuf))):
                for h in range(H_C):
                    cps.append(pltpu.make_async_copy(src.at[page, :, h, :], dst.at[slot, i, h],
                                                     sem.at[c, slot]))
        return cps

    slot = step % 2

    @pl.when(step == 0)
    def _():
        for cp in page_copies(step, slot):
            cp.start()

    @pl.when(step + 1 < n_total)
    def _():
        for cp in page_copies(step + 1, 1 - slot):
            cp.start()

    def bias_row():
        lane = lax.broadcasted_iota(jnp.int32, (1, HQ), 1)
        out = jnp.zeros((1, HQ), F32)
        for h in range(H_C):
            out = jnp.where(lane // DS == h, bias_ref[h], out)
        return out

    def page_rows(buf, i):
        return jnp.concatenate([buf[slot, i, h] for h in range(H_C)], axis=1).astype(BF16)

    @pl.when(p == 0)
    def _():
        qs = q_ref[...] * (DH_C ** -0.5)
        eye = (lax.broadcasted_iota(jnp.int32, (DS, DS), 0)
               == lax.broadcasted_iota(jnp.int32, (DS, DS), 1)).astype(F32)
        qT = lax.dot_general(qs, eye, TN_DIMS, precision=HIGHEST, preferred_element_type=F32)
        rep = jnp.concatenate([qT] * H_C, axis=1)
        r_head = lax.broadcasted_iota(jnp.int32, (D, HQ), 0) // DH_C
        c_head = lax.broadcasted_iota(jnp.int32, (D, HQ), 1) // DS
        qbd = jnp.where(r_head == c_head, rep, 0.0).astype(BF16)
        qbd_ref[...] = qbd
        z = jnp.dot(kn_ref[...].astype(BF16), qbd, preferred_element_type=F32) + bias_row()
        jrow = lax.broadcasted_iota(jnp.int32, (DS, HQ), 0)
        tq = lax.broadcasted_iota(jnp.int32, (DS, HQ), 1) % DS
        mask = jrow < tq
        sp = jnp.where(mask, _softplus(z), 0.0)
        triu = (lax.broadcasted_iota(jnp.int32, (DS, DS), 1)
                > lax.broadcasted_iota(jnp.int32, (DS, DS), 0)).astype(F32)
        aft = jnp.dot(triu, sp, precision=HIGHEST, preferred_element_type=F32)
        a = jnp.where(mask, jnp.exp(z - sp - aft), 0.0)
        acc_ref[...] = lax.dot_general(a.astype(BF16), vn_ref[...].astype(BF16), TN_DIMS,
                                       preferred_element_type=F32)
        R_ref[...] = jnp.sum(sp, axis=0, keepdims=True)

    for cp in page_copies(step, slot):
        cp.wait()
    kp = jnp.concatenate([page_rows(kbuf, i) for i in range(PP)], axis=0)
    vp = jnp.concatenate([page_rows(vbuf, i) for i in range(PP)], axis=0)
    z = jnp.dot(kp, qbd_ref[...], preferred_element_type=F32) + bias_row()
    sp = _softplus(z)
    hi = sp.astype(BF16)
    lo = (sp - hi.astype(F32)).astype(BF16)
    triu = (lax.broadcasted_iota(jnp.int32, (NK, NK), 1)
            > lax.broadcasted_iota(jnp.int32, (NK, NK), 0)).astype(BF16)
    aft = (jnp.dot(triu, hi, preferred_element_type=F32)
           + jnp.dot(triu, lo, preferred_element_type=F32) + R_ref[...])
    a = jnp.exp(z - sp - aft)
    acc_ref[...] += lax.dot_general(a.astype(BF16), vp, TN_DIMS, preferred_element_type=F32)
    R_ref[...] = aft[0:1, :] + sp[0:1, :]

    @pl.when(p == n_steps - 1)
    def _():
        for h in range(H_C):
            o_ref[:, h * DH_C:(h + 1) * DH_C] = acc_ref[h * DS:(h + 1) * DS,
                                                        h * DH_C:(h + 1) * DH_C].astype(o_ref.dtype)


def sb_sample_attn(q, kv_new, bias, cache_k, cache_v, layer, page_table, dst, DB, DS, row_block0):
    n_layers, n_pool = cache_k.shape[:2]
    n_pages = page_table.shape[1]
    PP = 2 if n_pages % 2 == 0 else 1
    n_steps = n_pages // PP
    D = H_C * DH_C
    kc = cache_k.reshape(n_layers * n_pool, PAGE_SIZE, H_C, DH_C)
    vc = cache_v.reshape(n_layers * n_pool, PAGE_SIZE, H_C, DH_C)
    buf = pltpu.VMEM((2, PP, H_C, PAGE_SIZE, DH_C), F32)
    grid_spec = pltpu.PrefetchScalarGridSpec(
        num_scalar_prefetch=1,
        grid=(DB, n_steps),
        in_specs=[
            pl.BlockSpec(memory_space=pltpu.SMEM),
            pl.BlockSpec((DS, D), lambda b, p, pt: (row_block0 + b, 0)),
            pl.BlockSpec((DS, D), lambda b, p, pt: (b, 0)),
            pl.BlockSpec((DS, D), lambda b, p, pt: (b, 1)),
            pl.BlockSpec(memory_space=pl.ANY),
            pl.BlockSpec(memory_space=pl.ANY),
            pl.BlockSpec(memory_space=pl.ANY),
        ],
        out_specs=pl.BlockSpec((DS, D), lambda b, p, pt: (row_block0 + b, 0)),
        scratch_shapes=[pltpu.VMEM((D, H_C * DS), BF16),
                        pltpu.VMEM((1, H_C * DS), F32),
                        pltpu.VMEM((H_C * DS, D), F32),
                        buf, buf, pltpu.SemaphoreType.DMA((2, 2))],
    )
    return pl.pallas_call(
        functools.partial(_sb_sample_kernel, DS=DS, n_steps=n_steps, n_pages=n_pages, PP=PP,
                          page0=layer * n_pool),
        grid_spec=grid_spec,
        out_shape=jax.ShapeDtypeStruct(dst.shape, dst.dtype),
        input_output_aliases={7: 0},
        compiler_params=_params("arbitrary", "arbitrary"),
        name="sb_sample_attn",
    )(page_table, bias, q, kv_new, kv_new, kc, vc, dst)


def _top_desc(x, n):
    rows = lax.broadcasted_iota(jnp.int32, x.shape, 0)
    big = jnp.int32(x.shape[0])
    vals = []
    for _ in range(n):
        m = jnp.max(x, axis=0, keepdims=True)
        first = jnp.min(jnp.where(x == m, rows, big), axis=0, keepdims=True)
        x = jnp.where(rows == first, -jnp.inf, x)
        vals.append(m)
    return vals


def _top_desc_distinct(x, n, n_pad):
    vals = []
    for _ in range(n):
        m = jnp.max(x, axis=0, keepdims=True)
        x = jnp.where(x == m, -jnp.inf, x)
        vals.append(m)
    dropped = jnp.sum(jnp.where(x == -jnp.inf, 1.0, 0.0), axis=0, keepdims=True)
    return vals, jnp.where(dropped != float(n + n_pad), 1.0, 0.0)


def _peer_route_kernel(q_ref, keys_ref, s1_ref, s2_ref, st_ref, *, TR):
    K = PEER_TOPK
    half = PEER_DQ // 2

    def run(h, exact):
        def top(x, n_pad):
            if exact:
                return _top_desc(x, K + 1), None
            return _top_desc_distinct(x, K + 1, n_pad)

        flags = jnp.zeros((1, TR), F32)
        tops = []
        for c, s_ref in enumerate((s1_ref, s2_ref)):
            if exact:
                s = s_ref[h]
            else:
                hc = 2 * h + c
                qh = q_ref[:, hc * half:(hc + 1) * half].astype(BF16)
                s = lax.dot_general(keys_ref[hc], qh, NT_DIMS, preferred_element_type=F32)
                s_ref[h] = s
            vals, bad = top(s, 0)
            tops.append(vals)
            if bad is not None:
                flags = jnp.maximum(flags, bad)
        v1, v2 = tops
        cand = []
        for a in range(K + 1):
            cand.extend(v1[a] + v2[b] for b in range((K + 1) // (a + 1)))
        pad = (-len(cand)) % SUBLANE
        cand = jnp.concatenate(cand + [jnp.full((pad, TR), -jnp.inf, F32)], axis=0)
        vals, bad = top(cand, pad)
        if bad is not None:
            flags = jnp.maximum(flags, bad)
        thr = 0.5 * (vals[K - 1] + vals[K])
        m = v1[0] + v2[0]
        zsum = jnp.sum(jnp.where(cand >= thr, jnp.exp(cand - m), 0.0), axis=0, keepdims=True)
        st_ref[h] = jnp.concatenate(
            [thr, v1[0], v2[0], 1.0 / zsum, jnp.zeros((4, TR), F32)], axis=0)
        return flags

    flags = [run(h, exact=False) for h in range(PEER_HEADS)]
    for h in range(PEER_HEADS):
        @pl.when(jnp.max(flags[h]) > 0.0)
        def _(h=h):
            run(h, exact=True)


def peer_route(q, keys_bf16, TR):
    T = q.shape[0]
    s_spec = pl.BlockSpec((PEER_HEADS, PEER_NKEYS, TR), lambda i: (0, 0, i))
    s_shape = jax.ShapeDtypeStruct((PEER_HEADS, PEER_NKEYS, T), F32)
    return pl.pallas_call(
        functools.partial(_peer_route_kernel, TR=TR),
        grid=(T // TR,),
        in_specs=[pl.BlockSpec((TR, PEER_HEADS * PEER_DQ), lambda i: (i, 0)),
                  pl.BlockSpec((2 * PEER_HEADS, PEER_NKEYS, PEER_DQ // 2), lambda i: (0, 0, 0))],
        out_specs=[s_spec, s_spec, pl.BlockSpec((PEER_HEADS, 8, TR), lambda i: (0, 0, i))],
        out_shape=[s_shape, s_shape, jax.ShapeDtypeStruct((PEER_HEADS, 8, T), F32)],
        compiler_params=_params("parallel"),
        name="peer_route",
    )(q, keys_bf16)


GATE_ROWS = 32
PEER_I1_BLOCK = SUBLANE
GELU_C0 = 0.7978845608028654
GELU_C1 = 0.7978845608028654 * 0.044715


def _peer_dense_kernel(h_ref, u_ref, vt_ref, s1_ref, s2_ref, st_ref, o_ref, e2_ref,
                       a0_ref, a1_ref, p0_ref, p1_ref, *, R, TB):
    j = pl.program_id(1)

    @pl.when(j == 0)
    def _():
        o_ref[...] = jnp.zeros_like(o_ref)
        for h in range(PEER_HEADS):
            e2_ref[h] = jnp.exp(s2_ref[h] - st_ref[h, 2:3, :]) * (0.5 * st_ref[h, 3:4, :])

    RH = R // 2
    acc = None
    for hf, (a_ref, p_ref) in enumerate(((a0_ref, p0_ref), (a1_ref, p1_ref))):
        e0 = hf * RH * PEER_NKEYS
        a_ref[...] = lax.dot_general(u_ref[e0:e0 + RH * PEER_NKEYS, :], h_ref[...], NT_DIMS,
                                     preferred_element_type=F32)
        for r in range(RH):
            rr = hf * RH + r
            for lc in range(TB // LANE):
                lanes = slice(lc * LANE, (lc + 1) * LANE)
                rowv = []
                for h in range(PEER_HEADS):
                    s1 = s1_ref[h, rr:rr + 1, lanes]
                    rowv.append((st_ref[h, 0:1, lanes] - s1, jnp.exp(s1 - st_ref[h, 1:2, lanes])))
                for g in range(PEER_NKEYS // GATE_ROWS):
                    sub = slice(g * GATE_ROWS, (g + 1) * GATE_ROWS)
                    gate = jnp.zeros((GATE_ROWS, LANE), F32)
                    for h in range(PEER_HEADS):
                        need, e1 = rowv[h]
                        gate = gate + jnp.where(s2_ref[h, sub, lanes] >= need, e1 * e2_ref[h, sub, lanes], 0.0)
                    rows = slice(r * PEER_NKEYS + g * GATE_ROWS, r * PEER_NKEYS + (g + 1) * GATE_ROWS)
                    a = a_ref[rows, lanes]
                    t = jnp.tanh(a * (GELU_C0 + GELU_C1 * (a * a)))
                    p_ref[rows, lanes] = (gate * (a + a * t)).astype(BF16)
        part = jnp.dot(vt_ref[:, e0:e0 + RH * PEER_NKEYS], p_ref[...], preferred_element_type=F32)
        acc = part if acc is None else acc + part
    o_ref[...] += acc


def peer_dense(h, u_bf16, vt_bf16, s1, s2, st, TB):
    T, D = h.shape
    R = PEER_I1_BLOCK
    half_shape = (R // 2 * PEER_NKEYS, TB)
    return pl.pallas_call(
        functools.partial(_peer_dense_kernel, R=R, TB=TB),
        grid=(T // TB, PEER_NKEYS // R),
        in_specs=[pl.BlockSpec((TB, D), lambda i, j: (i, 0)),
                  pl.BlockSpec((R * PEER_NKEYS, D), lambda i, j: (j, 0)),
                  pl.BlockSpec((D, R * PEER_NKEYS), lambda i, j: (0, j)),
                  pl.BlockSpec((PEER_HEADS, R, TB), lambda i, j: (0, j, i)),
                  pl.BlockSpec((PEER_HEADS, PEER_NKEYS, TB), lambda i, j: (0, 0, i)),
                  pl.BlockSpec((PEER_HEADS, 8, TB), lambda i, j: (0, 0, i))],
        out_specs=pl.BlockSpec((D, TB), lambda i, j: (0, i)),
        out_shape=jax.ShapeDtypeStruct((D, T), F32),
        scratch_shapes=[pltpu.VMEM((PEER_HEADS, PEER_NKEYS, TB), F32),
                        pltpu.VMEM(half_shape, F32), pltpu.VMEM(half_shape, F32),
                        pltpu.VMEM(half_shape, BF16), pltpu.VMEM(half_shape, BF16)],
        compiler_params=_params("parallel", "arbitrary"),
        name="peer_dense",
    )(h, u_bf16, vt_bf16, s1, s2, st)


def peer_layer(x, g, wq, keys, u, v_all, layer, TB):
    q, hn = rms_matmul(x, g, wq.astype(BF16), TB, 512, emit_h=True)
    keys_b = keys.reshape(2 * PEER_HEADS, PEER_NKEYS, PEER_DQ // 2).astype(BF16)
    s1, s2, st = peer_route(q, keys_b, 256)
    yt = peer_dense(hn, u.astype(BF16), transpose_cast(v_all, layer, 512, BF16), s1, s2, st, TB)
    return transpose_add(x, yt, TB)


def _pick_tile(n, cap, mult):
    best = mult
    for t in range(mult, cap + 1, mult):
        if n % t == 0:
            best = t
    return best


def kernel(x_prompt, x_sample, state_gla, state_conv, cache_k, cache_v, page_table, meta_tokens, norm_mix, norm_ffn, norm_final, w_in_even, w_gate_lr, b_gate_lr, gla_norm, conv_w, conv_b, conv_norm_g, conv_norm_b, w_out_even, w_qkv_odd, w_out_odd, sb_bias, peer_wq, peer_keys, peer_u, peer_v):
    B, SEQ, D = x_prompt.shape
    DB, DS, _ = x_sample.shape
    depth = norm_mix.shape[0]
    L = N_META + SEQ
    Tp, Ts = B * L, DB * DS
    T = Tp + Ts
    TB = 768 if T >= 768 else 256
    Tpad = -(-T // TB) * TB
    GC = 48 if L % 48 == 0 else 16
    assert L % GC == 0 and Tp % DS == 0 and DS % SUBLANE == 0 and DS <= 16
    TQ = _pick_tile(SEQ, 256, LANE)
    assert SEQ % TQ == 0

    xp = jnp.concatenate([jnp.broadcast_to(meta_tokens[None], (B, N_META, D)), x_prompt], axis=1)
    X = jnp.concatenate([xp.reshape(Tp, D), x_sample.reshape(Ts, D), jnp.zeros((Tpad - T, D), F32)], axis=0)

    gla_p, gla_s, conv_p, conv_s, k_p, v_p, k_s, v_s = [], [], [], [], [], [], [], []
    for layer in range(depth):
        i = layer // 2
        if layer % 2 == 0:
            w = w_in_even[i]
            n_main = 2 * QA + 2 * VA
            w_main = jnp.concatenate([w[:, :n_main], w[:, n_main + LOWRANK_A:]], axis=1).astype(BF16)
            w_lr = jnp.pad(w[:, n_main:n_main + LOWRANK_A], ((0, 0), (0, LANE - LOWRANK_A))).astype(BF16)
            P = rms_matmul(X, norm_mix[layer], w_main, TB, 512)
            LR = rms_matmul(X, norm_mix[layer], w_lr, TB, LANE)
            wlr = jnp.pad(w_gate_lr[i], ((0, LANE - LOWRANK_A), (0, 0))).astype(BF16)
            blr = b_gate_lr[i].reshape(1, QA)
            gn = gla_norm[i].reshape(1, DV_A)
            oa, sp_ = gla_mixer(P, LR, wlr, blr, gn, jnp.zeros((B, H_A, DK_A, DV_A), F32),
                                jnp.zeros((Tpad, VA), BF16), B, L // GC, GC, 0)
            oa, ss_ = gla_mixer(P, LR, wlr, blr, gn, state_gla[i], oa, DB, 1, DS, Tp // DS)
            cw = jnp.pad(conv_w[i], ((0, HALO - CONV_W), (0, 0)))
            cvec = [t.reshape(1, C_B) for t in (conv_b[i], conv_norm_g[i], conv_norm_b[i])]
            ob, tail_p = conv_mixer(P, jnp.zeros((B, HALO, C_B), F32), cw, *cvec,
                                    jnp.zeros((Tpad, C_B), BF16), B, L // GC, GC, 0)
            halo_s = jnp.pad(state_conv[i], ((0, 0), (HALO - (CONV_W - 1), 0), (0, 0)))
            ob, tail_s = conv_mixer(P, halo_s, cw, *cvec, ob, DB, 1, DS, Tp // DS)
            X = matmul_res([oa, ob], w_out_even[i].astype(BF16), X, TB, 512)
            gla_p.append(sp_)
            gla_s.append(ss_)
            conv_p.append(tail_p[:, HALO - (CONV_W - 1):])
            conv_s.append(tail_s[:, HALO - (CONV_W - 1):])
        else:
            HD = H_C * DH_C
            wqkv = w_qkv_odd[i].astype(BF16)
            q_all, kp_, vp_ = rms_qkv(X, norm_mix[layer], wqkv, TB, 512, Tp)
            kv_new = rms_matmul(X[Tp:T], norm_mix[layer], wqkv[:, HD:], Ts, 512)
            att = sb_prompt_attn(q_all, kp_, vp_, sb_bias[i], jnp.zeros((Tpad, HD), BF16), B, L, TQ)
            att = sb_sample_attn(q_all, kv_new, sb_bias[i], cache_k, cache_v, i, page_table, att, DB, DS, Tp // DS)
            X = matmul_res([att], w_out_odd[i].astype(BF16), X, TB, 512)
            k_p.append(kp_.reshape(B, L, H_C, DH_C))
            v_p.append(vp_.reshape(B, L, H_C, DH_C))
            k_s.append(kv_new[:, :HD].reshape(DB, DS, H_C, DH_C))
            v_s.append(kv_new[:, HD:].reshape(DB, DS, H_C, DH_C))
        X = peer_layer(X, norm_ffn[layer], peer_wq[layer], peer_keys[layer], peer_u[layer], peer_v, layer, TB)

    y_prompt = rmsnorm_seqs(X, norm_final, B, SEQ, N_META, L, TQ)
    y_sample = rmsnorm_rows(X[Tp:T], norm_final, Ts).reshape(DB, DS, D)
    return (y_prompt, y_sample, jnp.stack(gla_p), jnp.stack(gla_s), jnp.stack(conv_p), jnp.stack(conv_s),
            jnp.stack(k_p), jnp.stack(v_p), jnp.stack(k_s), jnp.stack(v_s))
```

```python
import functools

import jax
import jax.numpy as jnp
from jax import lax
from jax.experimental import pallas as pl
from jax.experimental.pallas import tpu as pltpu

F32 = jnp.float32
BF16 = jnp.bfloat16
HIGHEST = lax.Precision.HIGHEST

N_META = 16
EPS = 1e-6
H_A = 4
DK_A = 128
DV_A = 256
LOWRANK_A = 16
GATE_TAU = 16.0
C_B = 1024
CONV_W = 31
H_C = 16
DH_C = 128
PAGE_SIZE = 128
PEER_HEADS = 8
PEER_NKEYS = 128
PEER_DQ = 256
PEER_TOPK = 16
QA = H_A * DK_A
VA = H_A * DV_A

LANE = 128
SUBLANE = 8
HALO = 32
VMEM_LIMIT = 56 * 1024 * 1024
MATMUL_COLS = 1024

NT_DIMS = (((1,), (1,)), ((), ()))
TN_DIMS = (((0,), (0,)), ((), ()))


def _params(*sem):
    return pltpu.CompilerParams(dimension_semantics=sem, vmem_limit_bytes=VMEM_LIMIT)


def _softplus(x):
    return jnp.maximum(x, 0.0) + jnp.log(1.0 + jnp.exp(-jnp.abs(x)))


def _log_sigmoid(x):
    return -_softplus(-x)


def _sigmoid(x):
    return 1.0 / (1.0 + jnp.exp(-x))


def _silu(x):
    return x * _sigmoid(x)


def _gelu_tanh(x):
    c = 0.7978845608028654
    return 0.5 * x * (1.0 + jnp.tanh(c * (x + 0.044715 * (x * x * x))))


def _rms_matmul_kernel(x_ref, g_ref, w_ref, o_ref, *rest, emit_h):
    h_ref = rest[-1]

    @pl.when(pl.program_id(1) == 0)
    def _():
        x = x_ref[...]
        ms = jnp.mean(x * x, axis=-1, keepdims=True)
        h = (x * lax.rsqrt(ms + EPS) * g_ref[...]).astype(BF16)
        h_ref[...] = h
        if emit_h:
            rest[0][...] = h

    o_ref[...] = jnp.dot(h_ref[...], w_ref[...], preferred_element_type=F32)


def rms_matmul(x, g, w, tm, tn, emit_h=False):
    T, D = x.shape
    N = w.shape[1]
    out_shape = [jax.ShapeDtypeStruct((T, N), F32)]
    out_specs = [pl.BlockSpec((tm, tn), lambda i, j: (i, j))]
    if emit_h:
        out_shape.append(jax.ShapeDtypeStruct((T, D), BF16))
        out_specs.append(pl.BlockSpec((tm, D), lambda i, j: (i, 0)))
    res = pl.pallas_call(
        functools.partial(_rms_matmul_kernel, emit_h=emit_h),
        grid=(T // tm, N // tn),
        in_specs=[pl.BlockSpec((tm, D), lambda i, j: (i, 0)),
                  pl.BlockSpec((1, D), lambda i, j: (0, 0)),
                  pl.BlockSpec((D, tn), lambda i, j: (0, j))],
        out_specs=out_specs,
        out_shape=out_shape,
        scratch_shapes=[pltpu.VMEM((tm, D), BF16)],
        compiler_params=_params("parallel", "arbitrary"),
        name="rms_matmul",
    )(x, g.reshape(1, D), w)
    return res if emit_h else res[0]


def _matmul_res_kernel(*refs, widths):
    a_refs, (w_ref, r_ref, o_ref) = refs[:len(widths)], refs[len(widths):]
    acc = r_ref[...]
    k0 = 0
    for a_ref, kw in zip(a_refs, widths):
        acc = acc + jnp.dot(a_ref[...], w_ref[k0:k0 + kw, :], preferred_element_type=F32)
        k0 += kw
    o_ref[...] = acc


def matmul_res(parts, w, res, tm, tn):
    T = parts[0].shape[0]
    widths = tuple(a.shape[1] for a in parts)
    K, N = w.shape
    assert sum(widths) == K
    return pl.pallas_call(
        functools.partial(_matmul_res_kernel, widths=widths),
        grid=(T // tm, N // tn),
        in_specs=[pl.BlockSpec((tm, kw), lambda i, j: (i, 0)) for kw in widths]
        + [pl.BlockSpec((K, tn), lambda i, j: (0, j)),
           pl.BlockSpec((tm, tn), lambda i, j: (i, j))],
        out_specs=pl.BlockSpec((tm, tn), lambda i, j: (i, j)),
        out_shape=jax.ShapeDtypeStruct((T, N), F32),
        compiler_params=_params("parallel", "arbitrary"),
        name="matmul_res",
    )(*parts, w, res)


def _rms_qkv_kernel(x_ref, g_ref, w_ref, q_ref, k_ref, v_ref, h_ref, *, nq):
    j = pl.program_id(1)

    @pl.when(j == 0)
    def _():
        x = x_ref[...]
        ms = jnp.mean(x * x, axis=-1, keepdims=True)
        h_ref[...] = (x * lax.rsqrt(ms + EPS) * g_ref[...]).astype(BF16)

    res = jnp.dot(h_ref[...], w_ref[...], preferred_element_type=F32)

    @pl.when(j < nq)
    def _():
        q_ref[...] = res

    @pl.when((j >= nq) & (j < 2 * nq))
    def _():
        k_ref[...] = res

    @pl.when(j >= 2 * nq)
    def _():
        v_ref[...] = res


def rms_qkv(x, g, w, tm, tn, rows_kv):
    T, D = x.shape
    HD = w.shape[1] // 3
    nq = HD // tn
    clip = lambda j, part: jnp.clip(j - part * nq, 0, nq - 1)
    return pl.pallas_call(
        functools.partial(_rms_qkv_kernel, nq=nq),
        grid=(T // tm, 3 * nq),
        in_specs=[pl.BlockSpec((tm, D), lambda i, j: (i, 0)),
                  pl.BlockSpec((1, D), lambda i, j: (0, 0)),
                  pl.BlockSpec((D, tn), lambda i, j: (0, j))],
        out_specs=[pl.BlockSpec((tm, tn), lambda i, j: (i, clip(j, 0))),
                   pl.BlockSpec((tm, tn), lambda i, j: (i, clip(j, 1))),
                   pl.BlockSpec((tm, tn), lambda i, j: (i, clip(j, 2)))],
        out_shape=[jax.ShapeDtypeStruct((T, HD), F32),
                   jax.ShapeDtypeStruct((rows_kv, HD), F32),
                   jax.ShapeDtypeStruct((rows_kv, HD), F32)],
        scratch_shapes=[pltpu.VMEM((tm, D), BF16)],
        compiler_params=_params("parallel", "arbitrary"),
        name="rms_qkv",
    )(x, g.reshape(1, D), w)


def _rmsnorm_kernel(x_ref, g_ref, o_ref):
    x = x_ref[...]
    ms = jnp.mean(x * x, axis=-1, keepdims=True)
    o_ref[...] = (x * lax.rsqrt(ms + EPS) * g_ref[...]).reshape(o_ref.shape)


def rmsnorm_seqs(x, g, nb, seq_len, row0, row_stride, tr):
    D = x.shape[1]
    rows = lambda b, j: pl.multiple_of(row0 + b * row_stride + j * tr, SUBLANE)
    return pl.pallas_call(
        _rmsnorm_kernel,
        grid=(nb, seq_len // tr),
        in_specs=[pl.BlockSpec((pl.Element(tr), pl.Element(D)), lambda b, j: (rows(b, j), 0)),
                  pl.BlockSpec((1, D), lambda b, j: (0, 0))],
        out_specs=pl.BlockSpec((1, tr, D), lambda b, j: (b, j, 0)),
        out_shape=jax.ShapeDtypeStruct((nb, seq_len, D), F32),
        compiler_params=_params("parallel", "parallel"),
        name="rmsnorm_seqs",
    )(x, g.reshape(1, D))


def rmsnorm_rows(x, g, tm):
    T, D = x.shape
    return pl.pallas_call(
        _rmsnorm_kernel,
        grid=(T // tm,),
        in_specs=[pl.BlockSpec((tm, D), lambda i: (i, 0)),
                  pl.BlockSpec((1, D), lambda i: (0, 0))],
        out_specs=pl.BlockSpec((tm, D), lambda i: (i, 0)),
        out_shape=jax.ShapeDtypeStruct((T, D), F32),
        compiler_params=_params("parallel"),
        name="rmsnorm_final",
    )(x, g.reshape(1, D))


def _transpose_cast_kernel(x_ref, o_ref):
    o_ref[...] = x_ref[...].T.astype(o_ref.dtype)


def transpose_cast(xs, layer, tr, dtype):
    n_layers, N, D = xs.shape
    nblk = N // tr
    return pl.pallas_call(
        _transpose_cast_kernel,
        grid=(nblk,),
        in_specs=[pl.BlockSpec((tr, D), lambda i: (layer * nblk + i, 0))],
        out_specs=pl.BlockSpec((D, tr), lambda i: (0, i)),
        out_shape=jax.ShapeDtypeStruct((D, N), dtype),
        compiler_params=_params("parallel"),
        name="transpose_cast",
    )(xs.reshape(n_layers * N, D))


def _transpose_add_kernel(x_ref, yt_ref, o_ref):
    o_ref[...] = x_ref[...] + yt_ref[...].T


def transpose_add(x, yt, tb):
    T, D = x.shape
    return pl.pallas_call(
        _transpose_add_kernel,
        grid=(T // tb,),
        in_specs=[pl.BlockSpec((tb, D), lambda i: (i, 0)),
                  pl.BlockSpec((D, tb), lambda i: (0, i))],
        out_specs=pl.BlockSpec((tb, D), lambda i: (i, 0)),
        out_shape=jax.ShapeDtypeStruct((T, D), F32),
        compiler_params=_params("parallel"),
        name="transpose_add",
    )(x, yt)


def _gla_kernel(q_ref, k_ref, v_ref, g_ref, lr_ref, wlr_ref, blr_ref, gn_ref, s0_ref, dst_ref,
                o_ref, sout_ref, S_ref, *, C, sb):
    del dst_ref
    c = pl.program_id(1)

    @pl.when(c == 0)
    def _():
        S_ref[...] = s0_ref[0]

    x = jnp.dot(lr_ref[...].astype(BF16), wlr_ref[...], preferred_element_type=F32) + blr_ref[...]
    loga_all = _log_sigmoid(x) * (1.0 / GATE_TAU)
    row = lax.broadcasted_iota(jnp.int32, (C, C), 0)
    col = lax.broadcasted_iota(jnp.int32, (C, C), 1)
    tril = (row >= col).astype(F32)
    b_all = jnp.dot(tril, loga_all, precision=HIGHEST, preferred_element_type=F32)
    H = range(H_A)
    kk = [slice(h * DK_A, (h + 1) * DK_A) for h in H]
    vv = [slice(h * DV_A, (h + 1) * DV_A) for h in H]
    q = [q_ref[:, kk[h]] * (DK_A ** -0.5) for h in H]
    k = [k_ref[:, kk[h]] for h in H]
    v = [v_ref[:, vv[h]] for h in H]
    b = [b_all[:, kk[h]] for h in H]
    S = [S_ref[h] for h in H]
    hp = dict(precision=HIGHEST, preferred_element_type=F32)
    lp = dict(preferred_element_type=F32)

    o = [jnp.dot((q[h] * jnp.exp(b[h])).astype(BF16), S[h].astype(BF16), **lp) for h in H]
    tpos = lax.broadcasted_iota(jnp.int32, (sb, 1), 0)
    outs = [[] for _ in H]
    for i in range(C // sb):
        r0 = i * sb
        rows = slice(r0, r0 + sb)
        oi = [o[h][rows] for h in H]
        if i > 0:
            qt = [q[h][rows] * jnp.exp(b[h][rows] - b[h][r0 - 1:r0]) for h in H]
            kt = [k[h][:r0] * jnp.exp(b[h][r0 - 1:r0] - b[h][:r0]) for h in H]
            att = [lax.dot_general(qt[h].astype(BF16), kt[h].astype(BF16), NT_DIMS, **lp) for h in H]
            oi = [oi[h] + jnp.dot(att[h].astype(BF16), v[h][:r0].astype(BF16), **lp) for h in H]
        for s in range(sb):
            for h in H:
                bi = b[h][rows]
                dec = jnp.exp(jnp.minimum(bi - bi[s:s + 1], 0.0))
                a_col = jnp.sum(q[h][rows] * k[h][r0 + s:r0 + s + 1] * dec, axis=-1, keepdims=True)
                a_col = jnp.where(tpos >= s, a_col, 0.0)
                oi[h] = oi[h] + a_col * v[h][r0 + s:r0 + s + 1]
        for h in H:
            outs[h].append(oi[h])

    for h in H:
        oh = outs[h][0] if len(outs[h]) == 1 else jnp.concatenate(outs[h], axis=0)
        ms = jnp.mean(oh * oh, axis=-1, keepdims=True)
        on = oh * lax.rsqrt(ms + EPS) * gn_ref[...]
        o_ref[:, vv[h]] = (on * _silu(g_ref[:, vv[h]])).astype(o_ref.dtype)

    ones = jnp.ones((C, DV_A), F32)
    kd = [k[h] * jnp.exp(b[h][C - 1:C] - b[h]) for h in H]
    tot = [lax.dot_general(loga_all[:, kk[h]], ones, TN_DIMS, **hp) for h in H]
    upd = [lax.dot_general(kd[h], v[h], TN_DIMS, **hp) for h in H]
    for h in H:
        S_new = jnp.exp(tot[h]) * S[h] + upd[h]
        S_ref[h] = S_new
        sout_ref[0, h] = S_new


def gla_mixer(P, LR, wlr, blr, gn, s0, dst, nb, nchunks, C, row_block0):
    sb = min(16, C)
    rb = lambda b, c: row_block0 + b * nchunks + c
    return pl.pallas_call(
        functools.partial(_gla_kernel, C=C, sb=sb),
        grid=(nb, nchunks),
        in_specs=[
            pl.BlockSpec((C, QA), lambda b, c: (rb(b, c), 0)),
            pl.BlockSpec((C, QA), lambda b, c: (rb(b, c), 1)),
            pl.BlockSpec((C, VA), lambda b, c: (rb(b, c), 2 * QA // VA)),
            pl.BlockSpec((C, VA), lambda b, c: (rb(b, c), 2 * QA // VA + 1)),
            pl.BlockSpec((C, LANE), lambda b, c: (rb(b, c), 0)),
            pl.BlockSpec((LANE, QA), lambda b, c: (0, 0)),
            pl.BlockSpec((1, QA), lambda b, c: (0, 0)),
            pl.BlockSpec((1, DV_A), lambda b, c: (0, 0)),
            pl.BlockSpec((1, H_A, DK_A, DV_A), lambda b, c: (b, 0, 0, 0)),
            pl.BlockSpec(memory_space=pl.ANY),
        ],
        out_specs=[
            pl.BlockSpec((C, VA), lambda b, c: (rb(b, c), 0)),
            pl.BlockSpec((1, H_A, DK_A, DV_A), lambda b, c: (b, 0, 0, 0)),
        ],
        out_shape=[jax.ShapeDtypeStruct(dst.shape, dst.dtype),
                   jax.ShapeDtypeStruct((nb, H_A, DK_A, DV_A), F32)],
        scratch_shapes=[pltpu.VMEM((H_A, DK_A, DV_A), F32)],
        input_output_aliases={9: 0},
        compiler_params=_params("parallel", "arbitrary"),
        name="gla_mixer",
    )(P, P, P, P, LR, wlr, blr, gn, s0, dst)


def _conv_kernel(ga_ref, gb_ref, halo_ref, cw_ref, cb_ref, lg_ref, lb_ref, dst_ref, o_ref, tail_ref,
                 u_ref, sh_ref, *, R):
    del dst_ref

    @pl.when(pl.program_id(1) == 0)
    def _():
        u_ref[0:HALO] = halo_ref[0]

    u_ref[HALO:HALO + R] = ga_ref[...] * _sigmoid(gb_ref[...])
    span = HALO + R - SUBLANE
    for s in range(1, SUBLANE):
        sh_ref[s - 1, 0:span] = u_ref[s:s + span]
    acc = jnp.zeros((R, C_B), F32) + cb_ref[...]
    first = HALO - (CONV_W - 1)
    for w in range(CONV_W):
        a8, s = (first + w) // SUBLANE * SUBLANE, (first + w) % SUBLANE
        rows = u_ref[a8:a8 + R, :] if s == 0 else sh_ref[s - 1, a8:a8 + R, :]
        acc = acc + rows * cw_ref[w:w + 1, :]
    mu = jnp.mean(acc, axis=-1, keepdims=True)
    d = acc - mu
    var = jnp.mean(d * d, axis=-1, keepdims=True)
    y = d * lax.rsqrt(var + EPS) * lg_ref[...] + lb_ref[...]
    o_ref[...] = _silu(y).astype(o_ref.dtype)
    new_halo = u_ref[R:R + HALO]
    u_ref[0:HALO] = new_halo
    tail_ref[0] = new_halo


def conv_mixer(P, halo, cw, cb, lg, lb, dst, nb, nchunks, R, row_block0):
    rb = lambda b, c: row_block0 + b * nchunks + c
    ga_blk = (2 * QA + 2 * VA) // C_B
    return pl.pallas_call(
        functools.partial(_conv_kernel, R=R),
        grid=(nb, nchunks),
        in_specs=[
            pl.BlockSpec((R, C_B), lambda b, c: (rb(b, c), ga_blk)),
            pl.BlockSpec((R, C_B), lambda b, c: (rb(b, c), ga_blk + 1)),
            pl.BlockSpec((1, HALO, C_B), lambda b, c: (b, 0, 0)),
            pl.BlockSpec((HALO, C_B), lambda b, c: (0, 0)),
            pl.BlockSpec((1, C_B), lambda b, c: (0, 0)),
            pl.BlockSpec((1, C_B), lambda b, c: (0, 0)),
            pl.BlockSpec((1, C_B), lambda b, c: (0, 0)),
            pl.BlockSpec(memory_space=pl.ANY),
        ],
        out_specs=[
            pl.BlockSpec((R, C_B), lambda b, c: (rb(b, c), 0)),
            pl.BlockSpec((1, HALO, C_B), lambda b, c: (b, 0, 0)),
        ],
        out_shape=[jax.ShapeDtypeStruct(dst.shape, dst.dtype),
                   jax.ShapeDtypeStruct((nb, HALO, C_B), F32)],
        scratch_shapes=[pltpu.VMEM((HALO + R, C_B), F32),
                        pltpu.VMEM((SUBLANE - 1, HALO + R, C_B), F32)],
        input_output_aliases={7: 0},
        compiler_params=_params("parallel", "arbitrary"),
        name="conv_mixer",
    )(P, P, halo, cw, cb, lg, lb, dst)


def _sb_tiles(zs, Rs, vts, tri, mask):
    n = range(len(zs))
    sps = [_softplus(z) for z in zs]
    if mask is not None:
        sps = [jnp.where(mask, sp, 0.0) for sp in sps]
    his = [sp.astype(BF16) for sp in sps]
    los = [(sps[i] - his[i].astype(F32)).astype(BF16) for i in n]
    afts = [jnp.dot(his[i], tri, preferred_element_type=F32)
            + jnp.dot(los[i], tri, preferred_element_type=F32) + Rs[i] for i in n]
    aa = [jnp.exp(zs[i] - sps[i] - afts[i]) for i in n]
    if mask is not None:
        aa = [jnp.where(mask, a, 0.0) for a in aa]
    contribs = [jnp.dot(aa[i].astype(BF16), vts[i], preferred_element_type=F32) for i in n]
    return contribs, [afts[i][:, 0:1] + sps[i][:, 0:1] for i in n]


SB_HEADS_PER_STEP = 4


def _sb_prompt_kernel(bias_ref, q_ref, k_ref, v_ref, dst_ref, o_ref, *, L, TQ):
    del dst_ref
    M = N_META
    HPS = SB_HEADS_PER_STEP
    heads = range(HPS)
    hcol = [slice(i * DH_C, (i + 1) * DH_C) for i in heads]
    bias = [bias_ref[pl.program_id(1) * HPS + i] for i in heads]

    def tri_and_causal(n):
        row = lax.broadcasted_iota(jnp.int32, (n, n), 0)
        col = lax.broadcasted_iota(jnp.int32, (n, n), 1)
        return (row > col).astype(BF16), col < row

    def qrows(r0, n):
        return [(q_ref[r0:r0 + n, hcol[i]] * (DH_C ** -0.5)).astype(BF16) for i in heads]

    def tiles(qts, k0, n, Rs, tri, mask):
        zs = [lax.dot_general(qts[i], k_ref[pl.ds(k0, n), hcol[i]].astype(BF16), NT_DIMS,
                              preferred_element_type=F32) + bias[i] for i in heads]
        vts = [v_ref[pl.ds(k0, n), hcol[i]].astype(BF16) for i in heads]
        return _sb_tiles(zs, Rs, vts, tri, mask)

    tri_m, causal_m = tri_and_causal(M)
    tri_q, causal_q = tri_and_causal(TQ)

    accs, _ = tiles(qrows(0, M), 0, M, [jnp.zeros((M, 1), F32)] * HPS, tri_m, causal_m)
    for i in heads:
        o_ref[0:M, hcol[i]] = accs[i].astype(o_ref.dtype)

    for qi in range((L - M) // TQ):
        q0 = M + qi * TQ
        qts = qrows(q0, TQ)
        accs, Rs = tiles(qts, q0, TQ, [jnp.zeros((TQ, 1), F32)] * HPS, tri_q, causal_q)

        def body(j, carry, qts=qts, qi=qi):
            accs, Rs = carry
            k0 = pl.multiple_of(M + (qi - 1 - j) * TQ, SUBLANE)
            contribs, Rs = tiles(qts, k0, TQ, Rs, tri_q, None)
            return [a + c for a, c in zip(accs, contribs)], Rs

        if qi > 0:
            accs, Rs = lax.fori_loop(0, qi, body, (accs, Rs))
        contribs, _ = tiles(qts, 0, M, Rs, tri_m, None)
        for i in heads:
            o_ref[q0:q0 + TQ, hcol[i]] = (accs[i] + contribs[i]).astype(o_ref.dtype)


def sb_prompt_attn(q, k, v, bias, dst, nb, L, TQ):
    HPS = SB_HEADS_PER_STEP
    W = HPS * DH_C
    blk = pl.BlockSpec((L, W), lambda b, h: (b, h))
    return pl.pallas_call(
        functools.partial(_sb_prompt_kernel, L=L, TQ=TQ),
        grid=(nb, H_C // HPS),
        in_specs=[pl.BlockSpec(memory_space=pltpu.SMEM), blk, blk, blk, pl.BlockSpec(memory_space=pl.ANY)],
        out_specs=blk,
        out_shape=jax.ShapeDtypeStruct(dst.shape, dst.dtype),
        input_output_aliases={4: 0},
        compiler_params=_params("parallel", "parallel"),
        name="sb_prompt_attn",
    )(bias, q, k, v, dst)


def _sb_sample_kernel(pt_ref, bias_ref, q_ref, kn_ref, vn_ref, kc_ref, vc_ref, dst_ref, o_ref,
                      qbd_ref, R_ref, acc_ref, kbuf, vbuf, sem, *, DS, n_steps, n_pages, PP, page0):
    del dst_ref
    b = pl.program_id(0)
    p = pl.program_id(1)
    HQ = H_C * DS
    D = H_C * DH_C
    NK = PP * PAGE_SIZE
    step = b * n_steps + p
    n_total = pl.num_programs(0) * n_steps

    def page_copies(g, slot):
        gb = g // n_steps
        gp = g - gb * n_steps
        cps = []
        for i in range(PP):
            page = page0 + pt_ref[gb, n_pages - PP * (gp + 1) + i]
            for c, (src, dst) in enumerate(((kc_ref, kbuf), (vc_ref, vbuf))):
                for h in range(H_C):
                    cps.append(pltpu.make_async_copy(src.at[page, :, h, :], dst.at[slot, i, h],
                                                     sem.at[c, slot]))
        return cps

    slot = step % 2

    @pl.when(step == 0)
    def _():
        for cp in page_copies(step, slot):
            cp.start()

    @pl.when(step + 1 < n_total)
    def _():
        for cp in page_copies(step + 1, 1 - slot):
            cp.start()

    def bias_row():
        lane = lax.broadcasted_iota(jnp.int32, (1, HQ), 1)
        out = jnp.zeros((1, HQ), F32)
        for h in range(H_C):
            out = jnp.where(lane // DS == h, bias_ref[h], out)
        return out

    def page_rows(buf, i):
        return jnp.concatenate([buf[slot, i, h] for h in range(H_C)], axis=1).astype(BF16)

    @pl.when(p == 0)
    def _():
        qs = q_ref[...] * (DH_C ** -0.5)
        eye = (lax.broadcasted_iota(jnp.int32, (DS, DS), 0)
               == lax.broadcasted_iota(jnp.int32, (DS, DS), 1)).astype(F32)
        qT = lax.dot_general(qs, eye, TN_DIMS, precision=HIGHEST, preferred_element_type=F32)
        rep = jnp.concatenate([qT] * H_C, axis=1)
        r_head = lax.broadcasted_iota(jnp.int32, (D, HQ), 0) // DH_C
        c_head = lax.broadcasted_iota(jnp.int32, (D, HQ), 1) // DS
        qbd = jnp.where(r_head == c_head, rep, 0.0).astype(BF16)
        qbd_ref[...] = qbd
        z = jnp.dot(kn_ref[...].astype(BF16), qbd, preferred_element_type=F32) + bias_row()
        jrow = lax.broadcasted_iota(jnp.int32, (DS, HQ), 0)
        tq = lax.broadcasted_iota(jnp.int32, (DS, HQ), 1) % DS
        mask = jrow < tq
        sp = jnp.where(mask, _softplus(z), 0.0)
        triu = (lax.broadcasted_iota(jnp.int32, (DS, DS), 1)
                > lax.broadcasted_iota(jnp.int32, (DS, DS), 0)).astype(F32)
        aft = jnp.dot(triu, sp, precision=HIGHEST, preferred_element_type=F32)
        a = jnp.where(mask, jnp.exp(z - sp - aft), 0.0)
        acc_ref[...] = lax.dot_general(a.astype(BF16), vn_ref[...].astype(BF16), TN_DIMS,
                                       preferred_element_type=F32)
        R_ref[...] = jnp.sum(sp, axis=0, keepdims=True)

    for cp in page_copies(step, slot):
        cp.wait()
    kp = jnp.concatenate([page_rows(kbuf, i) for i in range(PP)], axis=0)
    vp = jnp.concatenate([page_rows(vbuf, i) for i in range(PP)], axis=0)
    z = jnp.dot(kp, qbd_ref[...], preferred_element_type=F32) + bias_row()
    sp = _softplus(z)
    hi = sp.astype(BF16)
    lo = (sp - hi.astype(F32)).astype(BF16)
    triu = (lax.broadcasted_iota(jnp.int32, (NK, NK), 1)
            > lax.broadcasted_iota(jnp.int32, (NK, NK), 0)).astype(BF16)
    aft = (jnp.dot(triu, hi, preferred_element_type=F32)
           + jnp.dot(triu, lo, preferred_element_type=F32) + R_ref[...])
    a = jnp.exp(z - sp - aft)
    acc_ref[...] += lax.dot_general(a.astype(BF16), vp, TN_DIMS, preferred_element_type=F32)
    R_ref[...] = aft[0:1, :] + sp[0:1, :]

    @pl.when(p == n_steps - 1)
    def _():
        for h in range(H_C):
            o_ref[:, h * DH_C:(h + 1) * DH_C] = acc_ref[h * DS:(h + 1) * DS,
                                                        h * DH_C:(h + 1) * DH_C].astype(o_ref.dtype)


def sb_sample_attn(q, kv_new, bias, cache_k, cache_v, layer, page_table, dst, DB, DS, row_block0):
    n_layers, n_pool = cache_k.shape[:2]
    n_pages = page_table.shape[1]
    PP = 2 if n_pages % 2 == 0 else 1
    n_steps = n_pages // PP
    D = H_C * DH_C
    kc = cache_k.reshape(n_layers * n_pool, PAGE_SIZE, H_C, DH_C)
    vc = cache_v.reshape(n_layers * n_pool, PAGE_SIZE, H_C, DH_C)
    buf = pltpu.VMEM((2, PP, H_C, PAGE_SIZE, DH_C), F32)
    grid_spec = pltpu.PrefetchScalarGridSpec(
        num_scalar_prefetch=1,
        grid=(DB, n_steps),
        in_specs=[
            pl.BlockSpec(memory_space=pltpu.SMEM),
            pl.BlockSpec((DS, D), lambda b, p, pt: (row_block0 + b, 0)),
            pl.BlockSpec((DS, D), lambda b, p, pt: (b, 0)),
            pl.BlockSpec((DS, D), lambda b, p, pt: (b, 1)),
            pl.BlockSpec(memory_space=pl.ANY),
            pl.BlockSpec(memory_space=pl.ANY),
            pl.BlockSpec(memory_space=pl.ANY),
        ],
        out_specs=pl.BlockSpec((DS, D), lambda b, p, pt: (row_block0 + b, 0)),
        scratch_shapes=[pltpu.VMEM((D, H_C * DS), BF16),
                        pltpu.VMEM((1, H_C * DS), F32),
                        pltpu.VMEM((H_C * DS, D), F32),
                        buf, buf, pltpu.SemaphoreType.DMA((2, 2))],
    )
    return pl.pallas_call(
        functools.partial(_sb_sample_kernel, DS=DS, n_steps=n_steps, n_pages=n_pages, PP=PP,
                          page0=layer * n_pool),
        grid_spec=grid_spec,
        out_shape=jax.ShapeDtypeStruct(dst.shape, dst.dtype),
        input_output_aliases={7: 0},
        compiler_params=_params("arbitrary", "arbitrary"),
        name="sb_sample_attn",
    )(page_table, bias, q, kv_new, kv_new, kc, vc, dst)


def _top_desc(x, n):
    rows = lax.broadcasted_iota(jnp.int32, x.shape, 0)
    big = jnp.int32(x.shape[0])
    vals = []
    for _ in range(n):
        m = jnp.max(x, axis=0, keepdims=True)
        first = jnp.min(jnp.where(x == m, rows, big), axis=0, keepdims=True)
        x = jnp.where(rows == first, -jnp.inf, x)
        vals.append(m)
    return vals


def _top_desc_distinct(x, n, n_pad):
    vals = []
    for _ in range(n):
        m = jnp.max(x, axis=0, keepdims=True)
        x = jnp.where(x == m, -jnp.inf, x)
        vals.append(m)
    dropped = jnp.sum(jnp.where(x == -jnp.inf, 1.0, 0.0), axis=0, keepdims=True)
    return vals, jnp.where(dropped != float(n + n_pad), 1.0, 0.0)


def _peer_route_kernel(q_ref, keys_ref, s1_ref, s2_ref, st_ref, *, TR):
    K = PEER_TOPK
    half = PEER_DQ // 2

    def run(h, exact):
        def top(x, n_pad):
            if exact:
                return _top_desc(x, K + 1), None
            return _top_desc_distinct(x, K + 1, n_pad)

        flags = jnp.zeros((1, TR), F32)
        tops = []
        for c, s_ref in enumerate((s1_ref, s2_ref)):
            if exact:
                s = s_ref[h]
            else:
                hc = 2 * h + c
                qh = q_ref[:, hc * half:(hc + 1) * half].astype(BF16)
                s = lax.dot_general(keys_ref[hc], qh, NT_DIMS, preferred_element_type=F32)
                s_ref[h] = s
            vals, bad = top(s, 0)
            tops.append(vals)
            if bad is not None:
                flags = jnp.maximum(flags, bad)
        v1, v2 = tops
        cand = []
        for a in range(K + 1):
            cand.extend(v1[a] + v2[b] for b in range((K + 1) // (a + 1)))
        pad = (-len(cand)) % SUBLANE
        cand = jnp.concatenate(cand + [jnp.full((pad, TR), -jnp.inf, F32)], axis=0)
        vals, bad = top(cand, pad)
        if bad is not None:
            flags = jnp.maximum(flags, bad)
        thr = 0.5 * (vals[K - 1] + vals[K])
        m = v1[0] + v2[0]
        zsum = jnp.sum(jnp.where(cand >= thr, jnp.exp(cand - m), 0.0), axis=0, keepdims=True)
        st_ref[h] = jnp.concatenate(
            [thr, v1[0], v2[0], 1.0 / zsum, jnp.zeros((4, TR), F32)], axis=0)
        return flags

    flags = [run(h, exact=False) for h in range(PEER_HEADS)]
    for h in range(PEER_HEADS):
        @pl.when(jnp.max(flags[h]) > 0.0)
        def _(h=h):
            run(h, exact=True)


def peer_route(q, keys_bf16, TR):
    T = q.shape[0]
    s_spec = pl.BlockSpec((PEER_HEADS, PEER_NKEYS, TR), lambda i: (0, 0, i))
    s_shape = jax.ShapeDtypeStruct((PEER_HEADS, PEER_NKEYS, T), F32)
    return pl.pallas_call(
        functools.partial(_peer_route_kernel, TR=TR),
        grid=(T // TR,),
        in_specs=[pl.BlockSpec((TR, PEER_HEADS * PEER_DQ), lambda i: (i, 0)),
                  pl.BlockSpec((2 * PEER_HEADS, PEER_NKEYS, PEER_DQ // 2), lambda i: (0, 0, 0))],
        out_specs=[s_spec, s_spec, pl.BlockSpec((PEER_HEADS, 8, TR), lambda i: (0, 0, i))],
        out_shape=[s_shape, s_shape, jax.ShapeDtypeStruct((PEER_HEADS, 8, T), F32)],
        compiler_params=_params("parallel"),
        name="peer_route",
    )(q, keys_bf16)


GATE_ROWS = 32
PEER_I1_BLOCK = SUBLANE
GELU_C0 = 0.7978845608028654
GELU_C1 = 0.7978845608028654 * 0.044715


def _peer_dense_kernel(h_ref, u_ref, vt_ref, s1_ref, s2_ref, st_ref, o_ref, e2_ref,
                       a0_ref, a1_ref, p0_ref, p1_ref, *, R, TB):
    j = pl.program_id(1)

    @pl.when(j == 0)
    def _():
        o_ref[...] = jnp.zeros_like(o_ref)
        for h in range(PEER_HEADS):
            e2_ref[h] = jnp.exp(s2_ref[h] - st_ref[h, 2:3, :]) * (0.5 * st_ref[h, 3:4, :])

    RH = R // 2
    acc = None
    for hf, (a_ref, p_ref) in enumerate(((a0_ref, p0_ref), (a1_ref, p1_ref))):
        e0 = hf * RH * PEER_NKEYS
        a_ref[...] = lax.dot_general(u_ref[e0:e0 + RH * PEER_NKEYS, :], h_ref[...], NT_DIMS,
                                     preferred_element_type=F32)
        for r in range(RH):
            rr = hf * RH + r
            for lc in range(TB // LANE):
                lanes = slice(lc * LANE, (lc + 1) * LANE)
                rowv = []
                for h in range(PEER_HEADS):
                    s1 = s1_ref[h, rr:rr + 1, lanes]
                    rowv.append((st_ref[h, 0:1, lanes] - s1, jnp.exp(s1 - st_ref[h, 1:2, lanes])))
                for g in range(PEER_NKEYS // GATE_ROWS):
                    sub = slice(g * GATE_ROWS, (g + 1) * GATE_ROWS)
                    gate = jnp.zeros((GATE_ROWS, LANE), F32)
                    for h in range(PEER_HEADS):
                        need, e1 = rowv[h]
                        gate = gate + jnp.where(s2_ref[h, sub, lanes] >= need, e1 * e2_ref[h, sub, lanes], 0.0)
                    rows = slice(r * PEER_NKEYS + g * GATE_ROWS, r * PEER_NKEYS + (g + 1) * GATE_ROWS)
                    a = a_ref[rows, lanes]
                    t = jnp.tanh(a * (GELU_C0 + GELU_C1 * (a * a)))
                    p_ref[rows, lanes] = (gate * (a + a * t)).astype(BF16)
        part = jnp.dot(vt_ref[:, e0:e0 + RH * PEER_NKEYS], p_ref[...], preferred_element_type=F32)
        acc = part if acc is None else acc + part
    o_ref[...] += acc


def peer_dense(h, u_bf16, vt_bf16, s1, s2, st, TB):
    T, D = h.shape
    R = PEER_I1_BLOCK
    half_shape = (R // 2 * PEER_NKEYS, TB)
    return pl.pallas_call(
        functools.partial(_peer_dense_kernel, R=R, TB=TB),
        grid=(T // TB, PEER_NKEYS // R),
        in_specs=[pl.BlockSpec((TB, D), lambda i, j: (i, 0)),
                  pl.BlockSpec((R * PEER_NKEYS, D), lambda i, j: (j, 0)),
                  pl.BlockSpec((D, R * PEER_NKEYS), lambda i, j: (0, j)),
                  pl.BlockSpec((PEER_HEADS, R, TB), lambda i, j: (0, j, i)),
                  pl.BlockSpec((PEER_HEADS, PEER_NKEYS, TB), lambda i, j: (0, 0, i)),
                  pl.BlockSpec((PEER_HEADS, 8, TB), lambda i, j: (0, 0, i))],
        out_specs=pl.BlockSpec((D, TB), lambda i, j: (0, i)),
        out_shape=jax.ShapeDtypeStruct((D, T), F32),
        scratch_shapes=[pltpu.VMEM((PEER_HEADS, PEER_NKEYS, TB), F32),
                        pltpu.VMEM(half_shape, F32), pltpu.VMEM(half_shape, F32),
                        pltpu.VMEM(half_shape, BF16), pltpu.VMEM(half_shape, BF16)],
        compiler_params=_params("parallel", "arbitrary"),
        name="peer_dense",
    )(h, u_bf16, vt_bf16, s1, s2, st)


def peer_layer(x, g, wq, keys, u, v_all, layer, TB):
    q, hn = rms_matmul(x, g, wq.astype(BF16), TB, MATMUL_COLS, emit_h=True)
    keys_b = keys.reshape(2 * PEER_HEADS, PEER_NKEYS, PEER_DQ // 2).astype(BF16)
    s1, s2, st = peer_route(q, keys_b, 256)
    yt = peer_dense(hn, u.astype(BF16), transpose_cast(v_all, layer, 512, BF16), s1, s2, st, TB)
    return transpose_add(x, yt, TB)


def _pick_tile(n, cap, mult):
    best = mult
    for t in range(mult, cap + 1, mult):
        if n % t == 0:
            best = t
    return best


def kernel(x_prompt, x_sample, state_gla, state_conv, cache_k, cache_v, page_table, meta_tokens, norm_mix, norm_ffn, norm_final, w_in_even, w_gate_lr, b_gate_lr, gla_norm, conv_w, conv_b, conv_norm_g, conv_norm_b, w_out_even, w_qkv_odd, w_out_odd, sb_bias, peer_wq, peer_keys, peer_u, peer_v):
    B, SEQ, D = x_prompt.shape
    DB, DS, _ = x_sample.shape
    depth = norm_mix.shape[0]
    L = N_META + SEQ
    Tp, Ts = B * L, DB * DS
    T = Tp + Ts
    TB = 768 if T >= 768 else 256
    Tpad = -(-T // TB) * TB
    GC = 48 if L % 48 == 0 else 16
    assert L % GC == 0 and Tp % DS == 0 and DS % SUBLANE == 0 and DS <= 16
    TQ = _pick_tile(SEQ, 256, LANE)
    assert SEQ % TQ == 0
    TN = MATMUL_COLS

    xp = jnp.concatenate([jnp.broadcast_to(meta_tokens[None], (B, N_META, D)), x_prompt], axis=1)
    X = jnp.concatenate([xp.reshape(Tp, D), x_sample.reshape(Ts, D), jnp.zeros((Tpad - T, D), F32)], axis=0)

    gla_p, gla_s, conv_p, conv_s, k_p, v_p, k_s, v_s = [], [], [], [], [], [], [], []
    for layer in range(depth):
        i = layer // 2
        if layer % 2 == 0:
            w = w_in_even[i]
            n_main = 2 * QA + 2 * VA
            w_main = jnp.concatenate([w[:, :n_main], w[:, n_main + LOWRANK_A:]], axis=1).astype(BF16)
            w_lr = jnp.pad(w[:, n_main:n_main + LOWRANK_A], ((0, 0), (0, LANE - LOWRANK_A))).astype(BF16)
            P = rms_matmul(X, norm_mix[layer], w_main, TB, TN)
            LR = rms_matmul(X, norm_mix[layer], w_lr, TB, LANE)
            wlr = jnp.pad(w_gate_lr[i], ((0, LANE - LOWRANK_A), (0, 0))).astype(BF16)
            blr = b_gate_lr[i].reshape(1, QA)
            gn = gla_norm[i].reshape(1, DV_A)
            oa, sp_ = gla_mixer(P, LR, wlr, blr, gn, jnp.zeros((B, H_A, DK_A, DV_A), F32),
                                jnp.zeros((Tpad, VA), BF16), B, L // GC, GC, 0)
            oa, ss_ = gla_mixer(P, LR, wlr, blr, gn, state_gla[i], oa, DB, 1, DS, Tp // DS)
            cw = jnp.pad(conv_w[i], ((0, HALO - CONV_W), (0, 0)))
            cvec = [t.reshape(1, C_B) for t in (conv_b[i], conv_norm_g[i], conv_norm_b[i])]
            ob, tail_p = conv_mixer(P, jnp.zeros((B, HALO, C_B), F32), cw, *cvec,
                                    jnp.zeros((Tpad, C_B), BF16), B, L // GC, GC, 0)
            halo_s = jnp.pad(state_conv[i], ((0, 0), (HALO - (CONV_W - 1), 0), (0, 0)))
            ob, tail_s = conv_mixer(P, halo_s, cw, *cvec, ob, DB, 1, DS, Tp // DS)
            X = matmul_res([oa, ob], w_out_even[i].astype(BF16), X, TB, TN)
            gla_p.append(sp_)
            gla_s.append(ss_)
            conv_p.append(tail_p[:, HALO - (CONV_W - 1):])
            conv_s.append(tail_s[:, HALO - (CONV_W - 1):])
        else:
            HD = H_C * DH_C
            wqkv = w_qkv_odd[i].astype(BF16)
            q_all, kp_, vp_ = rms_qkv(X, norm_mix[layer], wqkv, TB, TN, Tp)
            kv_new = rms_matmul(X[Tp:T], norm_mix[layer], wqkv[:, HD:], Ts, TN)
            att = sb_prompt_attn(q_all, kp_, vp_, sb_bias[i], jnp.zeros((Tpad, HD), BF16), B, L, TQ)
            att = sb_sample_attn(q_all, kv_new, sb_bias[i], cache_k, cache_v, i, page_table, att, DB, DS, Tp // DS)
            X = matmul_res([att], w_out_odd[i].astype(BF16), X, TB, TN)
            k_p.append(kp_.reshape(B, L, H_C, DH_C))
            v_p.append(vp_.reshape(B, L, H_C, DH_C))
            k_s.append(kv_new[:, :HD].reshape(DB, DS, H_C, DH_C))
            v_s.append(kv_new[:, HD:].reshape(DB, DS, H_C, DH_C))
        X = peer_layer(X, norm_ffn[layer], peer_wq[layer], peer_keys[layer], peer_u[layer], peer_v, layer, TB)

    y_prompt = rmsnorm_seqs(X, norm_final, B, SEQ, N_META, L, TQ)
    y_sample = rmsnorm_rows(X[Tp:T], norm_final, Ts).reshape(DB, DS, D)
    return (y_prompt, y_sample, jnp.stack(gla_p), jnp.stack(gla_s), jnp.stack(conv_p), jnp.stack(conv_s),
            jnp.stack(k_p), jnp.stack(v_p), jnp.stack(k_s), jnp.stack(v_s))
```

```python
import functools

import jax
import jax.numpy as jnp
from jax import lax
from jax.experimental import pallas as pl
from jax.experimental.pallas import tpu as pltpu

F32 = jnp.float32
BF16 = jnp.bfloat16
HIGHEST = lax.Precision.HIGHEST

N_META = 16
EPS = 1e-6
H_A = 4
DK_A = 128
DV_A = 256
LOWRANK_A = 16
GATE_TAU = 16.0
C_B = 1024
CONV_W = 31
H_C = 16
DH_C = 128
PAGE_SIZE = 128
PEER_HEADS = 8
PEER_NKEYS = 128
PEER_DQ = 256
PEER_TOPK = 16
QA = H_A * DK_A
VA = H_A * DV_A

LANE = 128
SUBLANE = 8
HALO = 32
VMEM_LIMIT = 56 * 1024 * 1024
MATMUL_COLS = 1024

NT_DIMS = (((1,), (1,)), ((), ()))
TN_DIMS = (((0,), (0,)), ((), ()))


def _params(*sem):
    return pltpu.CompilerParams(dimension_semantics=sem, vmem_limit_bytes=VMEM_LIMIT)


def _softplus(x):
    return jnp.maximum(x, 0.0) + jnp.log(1.0 + jnp.exp(-jnp.abs(x)))


def _log_sigmoid(x):
    return -_softplus(-x)


def _sigmoid(x):
    return 1.0 / (1.0 + jnp.exp(-x))


def _silu(x):
    return x * _sigmoid(x)


def _gelu_tanh(x):
    c = 0.7978845608028654
    return 0.5 * x * (1.0 + jnp.tanh(c * (x + 0.044715 * (x * x * x))))


def _rms_matmul_kernel(x_ref, g_ref, w_ref, o_ref, *rest, emit_h):
    h_ref = rest[-1]

    @pl.when(pl.program_id(1) == 0)
    def _():
        x = x_ref[...]
        ms = jnp.mean(x * x, axis=-1, keepdims=True)
        h = (x * lax.rsqrt(ms + EPS) * g_ref[...]).astype(BF16)
        h_ref[...] = h
        if emit_h:
            rest[0][...] = h

    o_ref[...] = jnp.dot(h_ref[...], w_ref[...], preferred_element_type=F32)


def rms_matmul(x, g, w, tm, tn, emit_h=False):
    T, D = x.shape
    N = w.shape[1]
    out_shape = [jax.ShapeDtypeStruct((T, N), F32)]
    out_specs = [pl.BlockSpec((tm, tn), lambda i, j: (i, j))]
    if emit_h:
        out_shape.append(jax.ShapeDtypeStruct((T, D), BF16))
        out_specs.append(pl.BlockSpec((tm, D), lambda i, j: (i, 0)))
    res = pl.pallas_call(
        functools.partial(_rms_matmul_kernel, emit_h=emit_h),
        grid=(T // tm, N // tn),
        in_specs=[pl.BlockSpec((tm, D), lambda i, j: (i, 0)),
                  pl.BlockSpec((1, D), lambda i, j: (0, 0)),
                  pl.BlockSpec((D, tn), lambda i, j: (0, j))],
        out_specs=out_specs,
        out_shape=out_shape,
        scratch_shapes=[pltpu.VMEM((tm, D), BF16)],
        compiler_params=_params("parallel", "arbitrary"),
        name="rms_matmul",
    )(x, g.reshape(1, D), w)
    return res if emit_h else res[0]


def _matmul_res_kernel(*refs, widths):
    a_refs, (w_ref, r_ref, o_ref) = refs[:len(widths)], refs[len(widths):]
    acc = r_ref[...]
    k0 = 0
    for a_ref, kw in zip(a_refs, widths):
        acc = acc + jnp.dot(a_ref[...], w_ref[k0:k0 + kw, :], preferred_element_type=F32)
        k0 += kw
    o_ref[...] = acc


def matmul_res(parts, w, res, tm, tn):
    T = parts[0].shape[0]
    widths = tuple(a.shape[1] for a in parts)
    K, N = w.shape
    assert sum(widths) == K
    return pl.pallas_call(
        functools.partial(_matmul_res_kernel, widths=widths),
        grid=(T // tm, N // tn),
        in_specs=[pl.BlockSpec((tm, kw), lambda i, j: (i, 0)) for kw in widths]
        + [pl.BlockSpec((K, tn), lambda i, j: (0, j)),
           pl.BlockSpec((tm, tn), lambda i, j: (i, j))],
        out_specs=pl.BlockSpec((tm, tn), lambda i, j: (i, j)),
        out_shape=jax.ShapeDtypeStruct((T, N), F32),
        compiler_params=_params("parallel", "arbitrary"),
        name="matmul_res",
    )(*parts, w, res)


def _rms_qkv_kernel(x_ref, g_ref, w_ref, q_ref, k_ref, v_ref, h_ref, *, nq):
    j = pl.program_id(1)

    @pl.when(j == 0)
    def _():
        x = x_ref[...]
        ms = jnp.mean(x * x, axis=-1, keepdims=True)
        h_ref[...] = (x * lax.rsqrt(ms + EPS) * g_ref[...]).astype(BF16)

    res = jnp.dot(h_ref[...], w_ref[...], preferred_element_type=F32)

    @pl.when(j < nq)
    def _():
        q_ref[...] = res

    @pl.when((j >= nq) & (j < 2 * nq))
    def _():
        k_ref[...] = res

    @pl.when(j >= 2 * nq)
    def _():
        v_ref[...] = res


def rms_qkv(x, g, w, tm, tn, rows_kv):
    T, D = x.shape
    HD = w.shape[1] // 3
    nq = HD // tn
    clip = lambda j, part: jnp.clip(j - part * nq, 0, nq - 1)
    return pl.pallas_call(
        functools.partial(_rms_qkv_kernel, nq=nq),
        grid=(T // tm, 3 * nq),
        in_specs=[pl.BlockSpec((tm, D), lambda i, j: (i, 0)),
                  pl.BlockSpec((1, D), lambda i, j: (0, 0)),
                  pl.BlockSpec((D, tn), lambda i, j: (0, j))],
        out_specs=[pl.BlockSpec((tm, tn), lambda i, j: (i, clip(j, 0))),
                   pl.BlockSpec((tm, tn), lambda i, j: (i, clip(j, 1))),
                   pl.BlockSpec((tm, tn), lambda i, j: (i, clip(j, 2)))],
        out_shape=[jax.ShapeDtypeStruct((T, HD), F32),
                   jax.ShapeDtypeStruct((rows_kv, HD), F32),
                   jax.ShapeDtypeStruct((rows_kv, HD), F32)],
        scratch_shapes=[pltpu.VMEM((tm, D), BF16)],
        compiler_params=_params("parallel", "arbitrary"),
        name="rms_qkv",
    )(x, g.reshape(1, D), w)


def _rmsnorm_kernel(x_ref, g_ref, o_ref):
    x = x_ref[...]
    ms = jnp.mean(x * x, axis=-1, keepdims=True)
    o_ref[...] = (x * lax.rsqrt(ms + EPS) * g_ref[...]).reshape(o_ref.shape)


def rmsnorm_seqs(x, g, nb, seq_len, row0, row_stride, tr):
    D = x.shape[1]
    rows = lambda b, j: pl.multiple_of(row0 + b * row_stride + j * tr, SUBLANE)
    return pl.pallas_call(
        _rmsnorm_kernel,
        grid=(nb, seq_len // tr),
        in_specs=[pl.BlockSpec((pl.Element(tr), pl.Element(D)), lambda b, j: (rows(b, j), 0)),
                  pl.BlockSpec((1, D), lambda b, j: (0, 0))],
        out_specs=pl.BlockSpec((1, tr, D), lambda b, j: (b, j, 0)),
        out_shape=jax.ShapeDtypeStruct((nb, seq_len, D), F32),
        compiler_params=_params("parallel", "parallel"),
        name="rmsnorm_seqs",
    )(x, g.reshape(1, D))


def rmsnorm_rows(x, g, tm):
    T, D = x.shape
    return pl.pallas_call(
        _rmsnorm_kernel,
        grid=(T // tm,),
        in_specs=[pl.BlockSpec((tm, D), lambda i: (i, 0)),
                  pl.BlockSpec((1, D), lambda i: (0, 0))],
        out_specs=pl.BlockSpec((tm, D), lambda i: (i, 0)),
        out_shape=jax.ShapeDtypeStruct((T, D), F32),
        compiler_params=_params("parallel"),
        name="rmsnorm_final",
    )(x, g.reshape(1, D))


def _transpose_cast_kernel(x_ref, o_ref):
    o_ref[...] = x_ref[...].T.astype(o_ref.dtype)


def transpose_cast(xs, layer, tr, dtype):
    n_layers, N, D = xs.shape
    nblk = N // tr
    return pl.pallas_call(
        _transpose_cast_kernel,
        grid=(nblk,),
        in_specs=[pl.BlockSpec((tr, D), lambda i: (layer * nblk + i, 0))],
        out_specs=pl.BlockSpec((D, tr), lambda i: (0, i)),
        out_shape=jax.ShapeDtypeStruct((D, N), dtype),
        compiler_params=_params("parallel"),
        name="transpose_cast",
    )(xs.reshape(n_layers * N, D))


def _transpose_add_kernel(x_ref, yt_ref, o_ref):
    o_ref[...] = x_ref[...] + yt_ref[...].T


def transpose_add(x, yt, tb):
    T, D = x.shape
    return pl.pallas_call(
        _transpose_add_kernel,
        grid=(T // tb,),
        in_specs=[pl.BlockSpec((tb, D), lambda i: (i, 0)),
                  pl.BlockSpec((D, tb), lambda i: (0, i))],
        out_specs=pl.BlockSpec((tb, D), lambda i: (i, 0)),
        out_shape=jax.ShapeDtypeStruct((T, D), F32),
        compiler_params=_params("parallel"),
        name="transpose_add",
    )(x, yt)


def _gla_kernel(q_ref, k_ref, v_ref, g_ref, lr_ref, wlr_ref, blr_ref, gn_ref, s0_ref, dst_ref,
                o_ref, sout_ref, S_ref, *, C, sb):
    del dst_ref
    c = pl.program_id(1)

    @pl.when(c == 0)
    def _():
        S_ref[...] = s0_ref[0]

    x = jnp.dot(lr_ref[...].astype(BF16), wlr_ref[...], preferred_element_type=F32) + blr_ref[...]
    loga_all = _log_sigmoid(x) * (1.0 / GATE_TAU)
    row = lax.broadcasted_iota(jnp.int32, (C, C), 0)
    col = lax.broadcasted_iota(jnp.int32, (C, C), 1)
    tril = (row >= col).astype(F32)
    b_all = jnp.dot(tril, loga_all, precision=HIGHEST, preferred_element_type=F32)
    H = range(H_A)
    kk = [slice(h * DK_A, (h + 1) * DK_A) for h in H]
    vv = [slice(h * DV_A, (h + 1) * DV_A) for h in H]
    q = [q_ref[:, kk[h]] * (DK_A ** -0.5) for h in H]
    k = [k_ref[:, kk[h]] for h in H]
    v = [v_ref[:, vv[h]] for h in H]
    b = [b_all[:, kk[h]] for h in H]
    S = [S_ref[h] for h in H]
    hp = dict(precision=HIGHEST, preferred_element_type=F32)
    lp = dict(preferred_element_type=F32)

    o = [jnp.dot((q[h] * jnp.exp(b[h])).astype(BF16), S[h].astype(BF16), **lp) for h in H]
    tpos = lax.broadcasted_iota(jnp.int32, (sb, 1), 0)
    outs = [[] for _ in H]
    for i in range(C // sb):
        r0 = i * sb
        rows = slice(r0, r0 + sb)
        oi = [o[h][rows] for h in H]
        if i > 0:
            qt = [q[h][rows] * jnp.exp(b[h][rows] - b[h][r0 - 1:r0]) for h in H]
            kt = [k[h][:r0] * jnp.exp(b[h][r0 - 1:r0] - b[h][:r0]) for h in H]
            att = [lax.dot_general(qt[h].astype(BF16), kt[h].astype(BF16), NT_DIMS, **lp) for h in H]
            oi = [oi[h] + jnp.dot(att[h].astype(BF16), v[h][:r0].astype(BF16), **lp) for h in H]
        for s in range(sb):
            for h in H:
                bi = b[h][rows]
                dec = jnp.exp(jnp.minimum(bi - bi[s:s + 1], 0.0))
                a_col = jnp.sum(q[h][rows] * k[h][r0 + s:r0 + s + 1] * dec, axis=-1, keepdims=True)
                a_col = jnp.where(tpos >= s, a_col, 0.0)
                oi[h] = oi[h] + a_col * v[h][r0 + s:r0 + s + 1]
        for h in H:
            outs[h].append(oi[h])

    for h in H:
        oh = outs[h][0] if len(outs[h]) == 1 else jnp.concatenate(outs[h], axis=0)
        ms = jnp.mean(oh * oh, axis=-1, keepdims=True)
        on = oh * lax.rsqrt(ms + EPS) * gn_ref[...]
        o_ref[:, vv[h]] = (on * _silu(g_ref[:, vv[h]])).astype(o_ref.dtype)

    ones = jnp.ones((C, DV_A), F32)
    kd = [k[h] * jnp.exp(b[h][C - 1:C] - b[h]) for h in H]
    tot = [lax.dot_general(loga_all[:, kk[h]], ones, TN_DIMS, **hp) for h in H]
    upd = [lax.dot_general(kd[h], v[h], TN_DIMS, **hp) for h in H]
    for h in H:
        S_new = jnp.exp(tot[h]) * S[h] + upd[h]
        S_ref[h] = S_new
        sout_ref[0, h] = S_new


def gla_mixer(P, LR, wlr, blr, gn, s0, dst, nb, nchunks, C, row_block0):
    sb = min(16, C)
    rb = lambda b, c: row_block0 + b * nchunks + c
    return pl.pallas_call(
        functools.partial(_gla_kernel, C=C, sb=sb),
        grid=(nb, nchunks),
        in_specs=[
            pl.BlockSpec((C, QA), lambda b, c: (rb(b, c), 0)),
            pl.BlockSpec((C, QA), lambda b, c: (rb(b, c), 1)),
            pl.BlockSpec((C, VA), lambda b, c: (rb(b, c), 2 * QA // VA)),
            pl.BlockSpec((C, VA), lambda b, c: (rb(b, c), 2 * QA // VA + 1)),
            pl.BlockSpec((C, LANE), lambda b, c: (rb(b, c), 0)),
            pl.BlockSpec((LANE, QA), lambda b, c: (0, 0)),
            pl.BlockSpec((1, QA), lambda b, c: (0, 0)),
            pl.BlockSpec((1, DV_A), lambda b, c: (0, 0)),
            pl.BlockSpec((1, H_A, DK_A, DV_A), lambda b, c: (b, 0, 0, 0)),
            pl.BlockSpec(memory_space=pl.ANY),
        ],
        out_specs=[
            pl.BlockSpec((C, VA), lambda b, c: (rb(b, c), 0)),
            pl.BlockSpec((1, H_A, DK_A, DV_A), lambda b, c: (b, 0, 0, 0)),
        ],
        out_shape=[jax.ShapeDtypeStruct(dst.shape, dst.dtype),
                   jax.ShapeDtypeStruct((nb, H_A, DK_A, DV_A), F32)],
        scratch_shapes=[pltpu.VMEM((H_A, DK_A, DV_A), F32)],
        input_output_aliases={9: 0},
        compiler_params=_params("parallel", "arbitrary"),
        name="gla_mixer",
    )(P, P, P, P, LR, wlr, blr, gn, s0, dst)


def _conv_kernel(ga_ref, gb_ref, halo_ref, cw_ref, cb_ref, lg_ref, lb_ref, dst_ref, o_ref, tail_ref,
                 u_ref, sh_ref, *, R):
    del dst_ref

    @pl.when(pl.program_id(1) == 0)
    def _():
        u_ref[0:HALO] = halo_ref[0]

    u_ref[HALO:HALO + R] = ga_ref[...] * _sigmoid(gb_ref[...])
    span = HALO + R - SUBLANE
    for s in range(1, SUBLANE):
        sh_ref[s - 1, 0:span] = u_ref[s:s + span]
    acc = jnp.zeros((R, C_B), F32) + cb_ref[...]
    first = HALO - (CONV_W - 1)
    for w in range(CONV_W):
        a8, s = (first + w) // SUBLANE * SUBLANE, (first + w) % SUBLANE
        rows = u_ref[a8:a8 + R, :] if s == 0 else sh_ref[s - 1, a8:a8 + R, :]
        acc = acc + rows * cw_ref[w:w + 1, :]
    mu = jnp.mean(acc, axis=-1, keepdims=True)
    d = acc - mu
    var = jnp.mean(d * d, axis=-1, keepdims=True)
    y = d * lax.rsqrt(var + EPS) * lg_ref[...] + lb_ref[...]
    o_ref[...] = _silu(y).astype(o_ref.dtype)
    new_halo = u_ref[R:R + HALO]
    u_ref[0:HALO] = new_halo
    tail_ref[0] = new_halo


def conv_mixer(P, halo, cw, cb, lg, lb, dst, nb, nchunks, R, row_block0):
    rb = lambda b, c: row_block0 + b * nchunks + c
    ga_blk = (2 * QA + 2 * VA) // C_B
    return pl.pallas_call(
        functools.partial(_conv_kernel, R=R),
        grid=(nb, nchunks),
        in_specs=[
            pl.BlockSpec((R, C_B), lambda b, c: (rb(b, c), ga_blk)),
            pl.BlockSpec((R, C_B), lambda b, c: (rb(b, c), ga_blk + 1)),
            pl.BlockSpec((1, HALO, C_B), lambda b, c: (b, 0, 0)),
            pl.BlockSpec((HALO, C_B), lambda b, c: (0, 0)),
            pl.BlockSpec((1, C_B), lambda b, c: (0, 0)),
            pl.BlockSpec((1, C_B), lambda b, c: (0, 0)),
            pl.BlockSpec((1, C_B), lambda b, c: (0, 0)),
            pl.BlockSpec(memory_space=pl.ANY),
        ],
        out_specs=[
            pl.BlockSpec((R, C_B), lambda b, c: (rb(b, c), 0)),
            pl.BlockSpec((1, HALO, C_B), lambda b, c: (b, 0, 0)),
        ],
        out_shape=[jax.ShapeDtypeStruct(dst.shape, dst.dtype),
                   jax.ShapeDtypeStruct((nb, HALO, C_B), F32)],
        scratch_shapes=[pltpu.VMEM((HALO + R, C_B), F32),
                        pltpu.VMEM((SUBLANE - 1, HALO + R, C_B), F32)],
        input_output_aliases={7: 0},
        compiler_params=_params("parallel", "arbitrary"),
        name="conv_mixer",
    )(P, P, halo, cw, cb, lg, lb, dst)


def _sb_tiles(zs, Rs, vts, tri, mask):
    n = range(len(zs))
    sps = [_softplus(z) for z in zs]
    if mask is not None:
        sps = [jnp.where(mask, sp, 0.0) for sp in sps]
    his = [sp.astype(BF16) for sp in sps]
    los = [(sps[i] - his[i].astype(F32)).astype(BF16) for i in n]
    afts = [jnp.dot(his[i], tri, preferred_element_type=F32)
            + jnp.dot(los[i], tri, preferred_element_type=F32) + Rs[i] for i in n]
    aa = [jnp.exp(zs[i] - sps[i] - afts[i]) for i in n]
    if mask is not None:
        aa = [jnp.where(mask, a, 0.0) for a in aa]
    contribs = [jnp.dot(aa[i].astype(BF16), vts[i], preferred_element_type=F32) for i in n]
    return contribs, [afts[i][:, 0:1] + sps[i][:, 0:1] for i in n]


SB_HEADS_PER_STEP = 4


def _sb_prompt_kernel(bias_ref, q_ref, k_ref, v_ref, dst_ref, o_ref, *, L, TQ):
    del dst_ref
    M = N_META
    HPS = SB_HEADS_PER_STEP
    heads = range(HPS)
    hcol = [slice(i * DH_C, (i + 1) * DH_C) for i in heads]
    bias = [bias_ref[pl.program_id(1) * HPS + i] for i in heads]

    def tri_and_causal(n):
        row = lax.broadcasted_iota(jnp.int32, (n, n), 0)
        col = lax.broadcasted_iota(jnp.int32, (n, n), 1)
        return (row > col).astype(BF16), col < row

    def qrows(r0, n):
        return [(q_ref[r0:r0 + n, hcol[i]] * (DH_C ** -0.5)).astype(BF16) for i in heads]

    def tiles(qts, k0, n, Rs, tri, mask):
        zs = [lax.dot_general(qts[i], k_ref[pl.ds(k0, n), hcol[i]].astype(BF16), NT_DIMS,
                              preferred_element_type=F32) + bias[i] for i in heads]
        vts = [v_ref[pl.ds(k0, n), hcol[i]].astype(BF16) for i in heads]
        return _sb_tiles(zs, Rs, vts, tri, mask)

    tri_m, causal_m = tri_and_causal(M)
    tri_q, causal_q = tri_and_causal(TQ)

    accs, _ = tiles(qrows(0, M), 0, M, [jnp.zeros((M, 1), F32)] * HPS, tri_m, causal_m)
    for i in heads:
        o_ref[0:M, hcol[i]] = accs[i].astype(o_ref.dtype)

    for qi in range((L - M) // TQ):
        q0 = M + qi * TQ
        qts = qrows(q0, TQ)
        accs, Rs = tiles(qts, q0, TQ, [jnp.zeros((TQ, 1), F32)] * HPS, tri_q, causal_q)

        def body(j, carry, qts=qts, qi=qi):
            accs, Rs = carry
            k0 = pl.multiple_of(M + (qi - 1 - j) * TQ, SUBLANE)
            contribs, Rs = tiles(qts, k0, TQ, Rs, tri_q, None)
            return [a + c for a, c in zip(accs, contribs)], Rs

        if qi > 0:
            accs, Rs = lax.fori_loop(0, qi, body, (accs, Rs))
        contribs, _ = tiles(qts, 0, M, Rs, tri_m, None)
        for i in heads:
            o_ref[q0:q0 + TQ, hcol[i]] = (accs[i] + contribs[i]).astype(o_ref.dtype)


def sb_prompt_attn(q, k, v, bias, dst, nb, L, TQ):
    HPS = SB_HEADS_PER_STEP
    W = HPS * DH_C
    blk = pl.BlockSpec((L, W), lambda b, h: (b, h))
    return pl.pallas_call(
        functools.partial(_sb_prompt_kernel, L=L, TQ=TQ),
        grid=(nb, H_C // HPS),
        in_specs=[pl.BlockSpec(memory_space=pltpu.SMEM), blk, blk, blk, pl.BlockSpec(memory_space=pl.ANY)],
        out_specs=blk,
        out_shape=jax.ShapeDtypeStruct(dst.shape, dst.dtype),
        input_output_aliases={4: 0},
        compiler_params=_params("parallel", "parallel"),
        name="sb_prompt_attn",
    )(bias, q, k, v, dst)


def _sb_sample_kernel(pt_ref, bias_ref, q_ref, kn_ref, vn_ref, kc_ref, vc_ref, dst_ref, o_ref,
                      qbd_ref, R_ref, acc_ref, kbuf, vbuf, sem, *, DS, n_steps, n_pages, PP, page0):
    del dst_ref
    b = pl.program_id(0)
    p = pl.program_id(1)
    HQ = H_C * DS
    D = H_C * DH_C
    NK = PP * PAGE_SIZE
    step = b * n_steps + p
    n_total = pl.num_programs(0) * n_steps

    def page_copies(g, slot):
        gb = g // n_steps
        gp = g - gb * n_steps
        cps = []
        for i in range(PP):
            page = page0 + pt_ref[gb, n_pages - PP * (gp + 1) + i]
            for c, (src, dst) in enumerate(((kc_ref, kbuf), (vc_ref, vbuf))):
                for h in range(H_C):
                    cps.append(pltpu.make_async_copy(src.at[page, :, h, :], dst.at[slot, i, h],
                                                     sem.at[c, slot]))
        return cps

    slot = step % 2

    @pl.when(step == 0)
    def _():
        for cp in page_copies(step, slot):
            cp.start()

    @pl.when(step + 1 < n_total)
    def _():
        for cp in page_copies(step + 1, 1 - slot):
            cp.start()

    def bias_row():
        lane = lax.broadcasted_iota(jnp.int32, (1, HQ), 1)
        out = jnp.zeros((1, HQ), F32)
        for h in range(H_C):
            out = jnp.where(lane // DS == h, bias_ref[h], out)
        return out

    def page_rows(buf, i):
        return jnp.concatenate([buf[slot, i, h] for h in range(H_C)], axis=1).astype(BF16)

    @pl.when(p == 0)
    def _():
        qs = q_ref[...] * (DH_C ** -0.5)
        eye = (lax.broadcasted_iota(jnp.int32, (DS, DS), 0)
               == lax.broadcasted_iota(jnp.int32, (DS, DS), 1)).astype(F32)
        qT = lax.dot_general(qs, eye, TN_DIMS, precision=HIGHEST, preferred_element_type=F32)
        rep = jnp.concatenate([qT] * H_C, axis=1)
        r_head = lax.broadcasted_iota(jnp.int32, (D, HQ), 0) // DH_C
        c_head = lax.broadcasted_iota(jnp.int32, (D, HQ), 1) // DS
        qbd = jnp.where(r_head == c_head, rep, 0.0).astype(BF16)
        qbd_ref[...] = qbd
        z = jnp.dot(kn_ref[...].astype(BF16), qbd, preferred_element_type=F32) + bias_row()
        jrow = lax.broadcasted_iota(jnp.int32, (DS, HQ), 0)
        tq = lax.broadcasted_iota(jnp.int32, (DS, HQ), 1) % DS
        mask = jrow < tq
        sp = jnp.where(mask, _softplus(z), 0.0)
        triu = (lax.broadcasted_iota(jnp.int32, (DS, DS), 1)
                > lax.broadcasted_iota(jnp.int32, (DS, DS), 0)).astype(F32)
        aft = jnp.dot(triu, sp, precision=HIGHEST, preferred_element_type=F32)
        a = jnp.where(mask, jnp.exp(z - sp - aft), 0.0)
        acc_ref[...] = lax.dot_general(a.astype(BF16), vn_ref[...].astype(BF16), TN_DIMS,
                                       preferred_element_type=F32)
        R_ref[...] = jnp.sum(sp, axis=0, keepdims=True)

    for cp in page_copies(step, slot):
        cp.wait()
    kp = jnp.concatenate([page_rows(kbuf, i) for i in range(PP)], axis=0)
    vp = jnp.concatenate([page_rows(vbuf, i) for i in range(PP)], axis=0)
    z = jnp.dot(kp, qbd_ref[...], preferred_element_type=F32) + bias_row()
    sp = _softplus(z)
    hi = sp.astype(BF16)
    lo = (sp - hi.astype(F32)).astype(BF16)
    triu = (lax.broadcasted_iota(jnp.int32, (NK, NK), 1)
            > lax.broadcasted_iota(jnp.int32, (NK, NK), 0)).astype(BF16)
    aft = (jnp.dot(triu, hi, preferred_element_type=F32)
           + jnp.dot(triu, lo, preferred_element_type=F32) + R_ref[...])
    a = jnp.exp(z - sp - aft)
    acc_ref[...] += lax.dot_general(a.astype(BF16), vp, TN_DIMS, preferred_element_type=F32)
    R_ref[...] = aft[0:1, :] + sp[0:1, :]

    @pl.when(p == n_steps - 1)
    def _():
        for h in range(H_C):
            o_ref[:, h * DH_C:(h + 1) * DH_C] = acc_ref[h * DS:(h + 1) * DS,
                                                        h * DH_C:(h + 1) * DH_C].astype(o_ref.dtype)


def sb_sample_attn(q, kv_new, bias, cache_k, cache_v, layer, page_table, dst, DB, DS, row_block0):
    n_layers, n_pool = cache_k.shape[:2]
    n_pages = page_table.shape[1]
    PP = 2 if n_pages % 2 == 0 else 1
    n_steps = n_pages // PP
    D = H_C * DH_C
    kc = cache_k.reshape(n_layers * n_pool, PAGE_SIZE, H_C, DH_C)
    vc = cache_v.reshape(n_layers * n_pool, PAGE_SIZE, H_C, DH_C)
    buf = pltpu.VMEM((2, PP, H_C, PAGE_SIZE, DH_C), F32)
    grid_spec = pltpu.PrefetchScalarGridSpec(
        num_scalar_prefetch=1,
        grid=(DB, n_steps),
        in_specs=[
            pl.BlockSpec(memory_space=pltpu.SMEM),
            pl.BlockSpec((DS, D), lambda b, p, pt: (row_block0 + b, 0)),
            pl.BlockSpec((DS, D), lambda b, p, pt: (b, 0)),
            pl.BlockSpec((DS, D), lambda b, p, pt: (b, 1)),
            pl.BlockSpec(memory_space=pl.ANY),
            pl.BlockSpec(memory_space=pl.ANY),
            pl.BlockSpec(memory_space=pl.ANY),
        ],
        out_specs=pl.BlockSpec((DS, D), lambda b, p, pt: (row_block0 + b, 0)),
        scratch_shapes=[pltpu.VMEM((D, H_C * DS), BF16),
                        pltpu.VMEM((1, H_C * DS), F32),
                        pltpu.VMEM((H_C * DS, D), F32),
                        buf, buf, pltpu.SemaphoreType.DMA((2, 2))],
    )
    return pl.pallas_call(
        functools.partial(_sb_sample_kernel, DS=DS, n_steps=n_steps, n_pages=n_pages, PP=PP,
                          page0=layer * n_pool),
        grid_spec=grid_spec,
        out_shape=jax.ShapeDtypeStruct(dst.shape, dst.dtype),
        input_output_aliases={7: 0},
        compiler_params=_params("arbitrary", "arbitrary"),
        name="sb_sample_attn",
    )(page_table, bias, q, kv_new, kv_new, kc, vc, dst)


def _top_desc(x, n):
    rows = lax.broadcasted_iota(jnp.int32, x.shape, 0)
    big = jnp.int32(x.shape[0])
    vals = []
    for _ in range(n):
        m = jnp.max(x, axis=0, keepdims=True)
        first = jnp.min(jnp.where(x == m, rows, big), axis=0, keepdims=True)
        x = jnp.where(rows == first, -jnp.inf, x)
        vals.append(m)
    return vals


def _top_desc_distinct(x, n, n_pad):
    vals = []
    for _ in range(n):
        m = jnp.max(x, axis=0, keepdims=True)
        x = jnp.where(x == m, -jnp.inf, x)
        vals.append(m)
    dropped = jnp.sum(jnp.where(x == -jnp.inf, 1.0, 0.0), axis=0, keepdims=True)
    return vals, jnp.where(dropped != float(n + n_pad), 1.0, 0.0)


def _peer_route_kernel(q_ref, keys_ref, s1_ref, s2_ref, st_ref, *, TR):
    K = PEER_TOPK
    half = PEER_DQ // 2

    def run(h, exact):
        def top(x, n_pad):
            if exact:
                return _top_desc(x, K + 1), None
            return _top_desc_distinct(x, K + 1, n_pad)

        flags = jnp.zeros((1, TR), F32)
        tops = []
        for c, s_ref in enumerate((s1_ref, s2_ref)):
            if exact:
                s = s_ref[h]
            else:
                hc = 2 * h + c
                qh = q_ref[:, hc * half:(hc + 1) * half].astype(BF16)
                s = lax.dot_general(keys_ref[hc], qh, NT_DIMS, preferred_element_type=F32)
                s_ref[h] = s
            vals, bad = top(s, 0)
            tops.append(vals)
            if bad is not None:
                flags = jnp.maximum(flags, bad)
        v1, v2 = tops
        cand = []
        for a in range(K + 1):
            cand.extend(v1[a] + v2[b] for b in range((K + 1) // (a + 1)))
        pad = (-len(cand)) % SUBLANE
        cand = jnp.concatenate(cand + [jnp.full((pad, TR), -jnp.inf, F32)], axis=0)
        vals, bad = top(cand, pad)
        if bad is not None:
            flags = jnp.maximum(flags, bad)
        thr = 0.5 * (vals[K - 1] + vals[K])
        m = v1[0] + v2[0]
        zsum = jnp.sum(jnp.where(cand >= thr, jnp.exp(cand - m), 0.0), axis=0, keepdims=True)
        st_ref[h] = jnp.concatenate(
            [thr, v1[0], v2[0], 1.0 / zsum, jnp.zeros((4, TR), F32)], axis=0)
        return flags

    flags = [run(h, exact=False) for h in range(PEER_HEADS)]
    for h in range(PEER_HEADS):
        @pl.when(jnp.max(flags[h]) > 0.0)
        def _(h=h):
            run(h, exact=True)


def peer_route(q, keys_bf16, TR):
    T = q.shape[0]
    s_spec = pl.BlockSpec((PEER_HEADS, PEER_NKEYS, TR), lambda i: (0, 0, i))
    s_shape = jax.ShapeDtypeStruct((PEER_HEADS, PEER_NKEYS, T), F32)
    return pl.pallas_call(
        functools.partial(_peer_route_kernel, TR=TR),
        grid=(T // TR,),
        in_specs=[pl.BlockSpec((TR, PEER_HEADS * PEER_DQ), lambda i: (i, 0)),
                  pl.BlockSpec((2 * PEER_HEADS, PEER_NKEYS, PEER_DQ // 2), lambda i: (0, 0, 0))],
        out_specs=[s_spec, s_spec, pl.BlockSpec((PEER_HEADS, 8, TR), lambda i: (0, 0, i))],
        out_shape=[s_shape, s_shape, jax.ShapeDtypeStruct((PEER_HEADS, 8, T), F32)],
        compiler_params=_params("parallel"),
        name="peer_route",
    )(q, keys_bf16)


GATE_ROWS = 32
PEER_I1_BLOCK = SUBLANE
GELU_C0 = 0.7978845608028654
GELU_C1 = 0.7978845608028654 * 0.044715


def _peer_dense_kernel(h_ref, u_ref, vt_ref, s1_ref, s2_ref, st_ref, o_ref, e2_ref,
                       a0_ref, a1_ref, p0_ref, p1_ref, *, R, TB):
    j = pl.program_id(1)

    @pl.when(j == 0)
    def _():
        o_ref[...] = jnp.zeros_like(o_ref)
        for h in range(PEER_HEADS):
            e2_ref[h] = jnp.exp(s2_ref[h] - st_ref[h, 2:3, :]) * (0.5 * st_ref[h, 3:4, :])

    RH = R // 2
    acc = None
    for hf, (a_ref, p_ref) in enumerate(((a0_ref, p0_ref), (a1_ref, p1_ref))):
        e0 = hf * RH * PEER_NKEYS
        a_ref[...] = lax.dot_general(u_ref[e0:e0 + RH * PEER_NKEYS, :], h_ref[...], NT_DIMS,
                                     preferred_element_type=F32)
        for r in range(RH):
            rr = hf * RH + r
            for lc in range(TB // LANE):
                lanes = slice(lc * LANE, (lc + 1) * LANE)
                rowv = []
                for h in range(PEER_HEADS):
                    s1 = s1_ref[h, rr:rr + 1, lanes]
                    rowv.append((st_ref[h, 0:1, lanes] - s1, jnp.exp(s1 - st_ref[h, 1:2, lanes])))
                for g in range(PEER_NKEYS // GATE_ROWS):
                    sub = slice(g * GATE_ROWS, (g + 1) * GATE_ROWS)
                    gate = jnp.zeros((GATE_ROWS, LANE), F32)
                    for h in range(PEER_HEADS):
                        need, e1 = rowv[h]
                        gate = gate + jnp.where(s2_ref[h, sub, lanes] >= need, e1 * e2_ref[h, sub, lanes], 0.0)
                    rows = slice(r * PEER_NKEYS + g * GATE_ROWS, r * PEER_NKEYS + (g + 1) * GATE_ROWS)
                    a = a_ref[rows, lanes]
                    t = jnp.tanh(a * (GELU_C0 + GELU_C1 * (a * a)))
                    p_ref[rows, lanes] = (gate * (a + a * t)).astype(BF16)
        part = jnp.dot(vt_ref[:, e0:e0 + RH * PEER_NKEYS], p_ref[...], preferred_element_type=F32)
        acc = part if acc is None else acc + part
    o_ref[...] += acc


def peer_dense(h, u_bf16, vt_bf16, s1, s2, st, TB):
    T, D = h.shape
    R = PEER_I1_BLOCK
    half_shape = (R // 2 * PEER_NKEYS, TB)
    return pl.pallas_call(
        functools.partial(_peer_dense_kernel, R=R, TB=TB),
        grid=(T // TB, PEER_NKEYS // R),
        in_specs=[pl.BlockSpec((TB, D), lambda i, j: (i, 0)),
                  pl.BlockSpec((R * PEER_NKEYS, D), lambda i, j: (j, 0)),
                  pl.BlockSpec((D, R * PEER_NKEYS), lambda i, j: (0, j)),
                  pl.BlockSpec((PEER_HEADS, R, TB), lambda i, j: (0, j, i)),
                  pl.BlockSpec((PEER_HEADS, PEER_NKEYS, TB), lambda i, j: (0, 0, i)),
                  pl.BlockSpec((PEER_HEADS, 8, TB), lambda i, j: (0, 0, i))],
        out_specs=pl.BlockSpec((D, TB), lambda i, j: (0, i)),
        out_shape=jax.ShapeDtypeStruct((D, T), F32),
        scratch_shapes=[pltpu.VMEM((PEER_HEADS, PEER_NKEYS, TB), F32),
                        pltpu.VMEM(half_shape, F32), pltpu.VMEM(half_shape, F32),
                        pltpu.VMEM(half_shape, BF16), pltpu.VMEM(half_shape, BF16)],
        compiler_params=_params("parallel", "arbitrary"),
        name="peer_dense",
    )(h, u_bf16, vt_bf16, s1, s2, st)


def peer_layer(x, g, wq, keys, u, v_all, layer, TB):
    q, hn = rms_matmul(x, g, wq.astype(BF16), TB, MATMUL_COLS, emit_h=True)
    keys_b = keys.reshape(2 * PEER_HEADS, PEER_NKEYS, PEER_DQ // 2).astype(BF16)
    s1, s2, st = peer_route(q, keys_b, 256)
    yt = peer_dense(hn, u.astype(BF16), transpose_cast(v_all, layer, 512, BF16), s1, s2, st, TB)
    return transpose_add(x, yt, TB)


def _pick_tile(n, cap, mult):
    best = mult
    for t in range(mult, cap + 1, mult):
        if n % t == 0:
            best = t
    return best


def kernel(x_prompt, x_sample, state_gla, state_conv, cache_k, cache_v, page_table, meta_tokens, norm_mix, norm_ffn, norm_final, w_in_even, w_gate_lr, b_gate_lr, gla_norm, conv_w, conv_b, conv_norm_g, conv_norm_b, w_out_even, w_qkv_odd, w_out_odd, sb_bias, peer_wq, peer_keys, peer_u, peer_v):
    B, SEQ, D = x_prompt.shape
    DB, DS, _ = x_sample.shape
    depth = norm_mix.shape[0]
    L = N_META + SEQ
    Tp, Ts = B * L, DB * DS
    T = Tp + Ts
    TB = 768 if T >= 768 else 256
    Tpad = -(-T // TB) * TB
    GC = 48 if L % 48 == 0 else 16
    assert L % GC == 0 and Tp % DS == 0 and DS % SUBLANE == 0 and DS <= 16
    TQ = _pick_tile(SEQ, 256, LANE)
    assert SEQ % TQ == 0
    TN = MATMUL_COLS

    xp = jnp.concatenate([jnp.broadcast_to(meta_tokens[None], (B, N_META, D)), x_prompt], axis=1)
    X = jnp.concatenate([xp.reshape(Tp, D), x_sample.reshape(Ts, D), jnp.zeros((Tpad - T, D), F32)], axis=0)

    gla_p, gla_s, conv_p, conv_s, k_p, v_p, k_s, v_s = [], [], [], [], [], [], [], []
    for layer in range(depth):
        i = layer // 2
        if layer % 2 == 0:
            w = w_in_even[i]
            n_main = 2 * QA + 2 * VA
            w_main = jnp.concatenate([w[:, :n_main], w[:, n_main + LOWRANK_A:]], axis=1).astype(BF16)
            w_lr = jnp.pad(w[:, n_main:n_main + LOWRANK_A], ((0, 0), (0, LANE - LOWRANK_A))).astype(BF16)
            P = rms_matmul(X, norm_mix[layer], w_main, TB, TN)
            LR = rms_matmul(X, norm_mix[layer], w_lr, TB, LANE)
            wlr = jnp.pad(w_gate_lr[i], ((0, LANE - LOWRANK_A), (0, 0))).astype(BF16)
            blr = b_gate_lr[i].reshape(1, QA)
            gn = gla_norm[i].reshape(1, DV_A)
            oa, sp_ = gla_mixer(P, LR, wlr, blr, gn, jnp.zeros((B, H_A, DK_A, DV_A), F32),
                                jnp.zeros((Tpad, VA), BF16), B, L // GC, GC, 0)
            oa, ss_ = gla_mixer(P, LR, wlr, blr, gn, state_gla[i], oa, DB, 1, DS, Tp // DS)
            cw = jnp.pad(conv_w[i], ((0, HALO - CONV_W), (0, 0)))
            cvec = [t.reshape(1, C_B) for t in (conv_b[i], conv_norm_g[i], conv_norm_b[i])]
            ob, tail_p = conv_mixer(P, jnp.zeros((B, HALO, C_B), F32), cw, *cvec,
                                    jnp.zeros((Tpad, C_B), BF16), B, L // GC, GC, 0)
            halo_s = jnp.pad(state_conv[i], ((0, 0), (HALO - (CONV_W - 1), 0), (0, 0)))
            ob, tail_s = conv_mixer(P, halo_s, cw, *cvec, ob, DB, 1, DS, Tp // DS)
            X = matmul_res([oa, ob], w_out_even[i].astype(BF16), X, TB, D)
            gla_p.append(sp_)
            gla_s.append(ss_)
            conv_p.append(tail_p[:, HALO - (CONV_W - 1):])
            conv_s.append(tail_s[:, HALO - (CONV_W - 1):])
        else:
            HD = H_C * DH_C
            wqkv = w_qkv_odd[i].astype(BF16)
            q_all, kp_, vp_ = rms_qkv(X, norm_mix[layer], wqkv, TB, TN, Tp)
            kv_new = rms_matmul(X[Tp:T], norm_mix[layer], wqkv[:, HD:], Ts, TN)
            att = sb_prompt_attn(q_all, kp_, vp_, sb_bias[i], jnp.zeros((Tpad, HD), BF16), B, L, TQ)
            att = sb_sample_attn(q_all, kv_new, sb_bias[i], cache_k, cache_v, i, page_table, att, DB, DS, Tp // DS)
            X = matmul_res([att], w_out_odd[i].astype(BF16), X, TB, D)
            k_p.append(kp_.reshape(B, L, H_C, DH_C))
            v_p.append(vp_.reshape(B, L, H_C, DH_C))
            k_s.append(kv_new[:, :HD].reshape(DB, DS, H_C, DH_C))
            v_s.append(kv_new[:, HD:].reshape(DB, DS, H_C, DH_C))
        X = peer_layer(X, norm_ffn[layer], peer_wq[layer], peer_keys[layer], peer_u[layer], peer_v, layer, TB)

    y_prompt = rmsnorm_seqs(X, norm_final, B, SEQ, N_META, L, TQ)
    y_sample = rmsnorm_rows(X[Tp:T], norm_final, Ts).reshape(DB, DS, D)
    return (y_prompt, y_sample, jnp.stack(gla_p), jnp.stack(gla_s), jnp.stack(conv_p), jnp.stack(conv_s),
            jnp.stack(k_p), jnp.stack(v_p), jnp.stack(k_s), jnp.stack(v_s))
```
